```python
import jax, jax.numpy as jnp
from jax import lax
import numpy as np

D_MODEL = 1024
BATCH = 8
SEQ = 4096
DEPTH = 1
DEC_BATCH = 32
DEC_SEQ = 2048
PAST_LEN = 128

MIX_WIDTH = D_MODEL
A_WIDTH = MIX_WIDTH // 2
A_HEADS = 4
A_HEAD_DIM = A_WIDTH // A_HEADS
CHUNK = 128
B_HEADS = 4
NOPE_DIM = 128
ROPE_DIM = 64
V_DIM = 128
QK_DIM = NOPE_DIM + ROPE_DIM
B_WIDTH = B_HEADS * V_DIM
Q_RANK = 384
KV_RANK = 256
IN_COLS = 2 * A_WIDTH + Q_RANK + KV_RANK + ROPE_DIM
D_FF = 4 * D_MODEL
Q_BLOCK = 128
ROPE_BASE = 10000.0
EPS = 1e-6
ATTN_SCALE = QK_DIM ** -0.5

kernel_name = "hymba_gmlp_mla_encoder"


def rms_norm(x, g):
    xf = x.astype(jnp.float32)
    y = xf * lax.rsqrt(jnp.mean(xf * xf, axis=-1, keepdims=True) + EPS)
    return (y * g.astype(jnp.float32)).astype(x.dtype)


def rope_tables(seq):
    inv = 1.0 / (ROPE_BASE ** (jnp.arange(0, ROPE_DIM, 2, dtype=jnp.float32) / ROPE_DIM))
    ang = jnp.arange(seq, dtype=jnp.float32)[:, None] * inv[None, :]
    return jnp.cos(ang), jnp.sin(ang)


def apply_rope(x, cos, sin):
    x1, x2 = jnp.split(x.astype(jnp.float32), 2, axis=-1)
    c = cos[None, :, None, :]
    s = sin[None, :, None, :]
    return jnp.concatenate([x1 * c - x2 * s, x1 * s + x2 * c], axis=-1).astype(x.dtype)


def chunked_spatial_gating(u, v, sgu_norm, w_spatial, b_spatial):
    bsz, seq, _ = u.shape
    n_chunks = seq // CHUNK
    v = rms_norm(v.reshape(bsz, seq, A_HEADS, A_HEAD_DIM), sgu_norm.reshape(A_HEADS, A_HEAD_DIM))
    v = v.reshape(bsz, n_chunks, CHUNK, A_HEADS, A_HEAD_DIM)
    s = jnp.einsum('hpq,bcqhd->bcphd', w_spatial, v) + b_spatial.T[None, None, :, :, None]
    return u * s.reshape(bsz, seq, A_WIDTH)


def latent_attention(c_q, c_kv, k_rope, q_norm, w_uq, kv_norm, w_ukv, cos, sin):
    bsz, seq, _ = c_q.shape
    q = jnp.einsum('bsr,rn->bsn', rms_norm(c_q, q_norm), w_uq).reshape(bsz, seq, B_HEADS, QK_DIM)
    kv = jnp.einsum('bsr,rn->bsn', rms_norm(c_kv, kv_norm), w_ukv).reshape(bsz, seq, B_HEADS, NOPE_DIM + V_DIM)
    q_nope, q_pe = q[..., :NOPE_DIM], q[..., NOPE_DIM:]
    k_nope, v = kv[..., :NOPE_DIM], kv[..., NOPE_DIM:]
    q_pe = apply_rope(q_pe, cos, sin)
    k_pe = apply_rope(k_rope[:, :, None, :], cos, sin)
    q = jnp.concatenate([q_nope, q_pe], axis=-1) * ATTN_SCALE
    k = jnp.concatenate([k_nope, jnp.broadcast_to(k_pe, (bsz, seq, B_HEADS, ROPE_DIM))], axis=-1)
    q_blocks = q.reshape(bsz, seq // Q_BLOCK, Q_BLOCK, B_HEADS, QK_DIM).transpose(1, 0, 2, 3, 4)

    def attend(qb):
        sc = jnp.einsum('bqhd,bkhd->bhqk', qb, k).astype(jnp.float32)
        p = jax.nn.softmax(sc, axis=-1).astype(v.dtype)
        return jnp.einsum('bhqk,bkhd->bqhd', p, v)

    o = lax.map(attend, q_blocks)
    return o.transpose(1, 0, 2, 3, 4).reshape(bsz, seq, B_WIDTH)


def encoder_layer(x, norm_mix, w_in, sgu_norm, w_spatial, b_spatial, q_norm, w_uq, kv_norm, w_ukv,
                  out_norm_a, out_norm_b, w_out, norm_ffn, w_ff1, w_ff2, cos, sin):
    h = rms_norm(x, norm_mix)
    z = jnp.einsum('bsd,dn->bsn', h, w_in)
    uv = jax.nn.gelu(z[..., :2 * A_WIDTH])
    u, v = uv[..., :A_WIDTH], uv[..., A_WIDTH:]
    o = 2 * A_WIDTH
    c_q = z[..., o:o + Q_RANK]
    o = o + Q_RANK
    c_kv = z[..., o:o + KV_RANK]
    o = o + KV_RANK
    k_rope = z[..., o:o + ROPE_DIM]
    y_a = chunked_spatial_gating(u, v, sgu_norm, w_spatial, b_spatial)
    y_b = latent_attention(c_q, c_kv, k_rope, q_norm, w_uq, kv_norm, w_ukv, cos, sin)
    y = jnp.concatenate([rms_norm(y_a, out_norm_a), rms_norm(y_b, out_norm_b)], axis=-1)
    x = x + jnp.einsum('bsm,md->bsd', y, w_out)
    h = rms_norm(x, norm_ffn)
    f = jnp.square(jax.nn.relu(jnp.einsum('bsd,df->bsf', h, w_ff1)))
    return x + jnp.einsum('bsf,fd->bsd', f, w_ff2)


def run_trunk(x, norm_mix, w_in, sgu_norm, w_spatial, b_spatial, q_norm, w_uq, kv_norm, w_ukv,
              out_norm_a, out_norm_b, w_out, norm_ffn, w_ff1, w_ff2, norm_final):
    cos, sin = rope_tables(x.shape[1])
    for l in range(DEPTH):
        x = encoder_layer(x, norm_mix[l], w_in[l], sgu_norm[l], w_spatial[l], b_spatial[l],
                          q_norm[l], w_uq[l], kv_norm[l], w_ukv[l], out_norm_a[l], out_norm_b[l],
                          w_out[l], norm_ffn[l], w_ff1[l], w_ff2[l], cos, sin)
    return rms_norm(x, norm_final)


def setup_inputs(seed: int = 0) -> dict:
    key = jax.random.key(seed)
    ks = jax.random.split(key, 20)
    f32 = jnp.float32

    def nrm(k, shape, scale):
        return jax.random.normal(k, shape, dtype=f32) * scale

    def gain(k, shape):
        return 1.0 + 0.02 * jax.random.normal(k, shape, dtype=f32)

    L = DEPTH
    return {
        "x_prompt": jax.random.normal(ks[0], (BATCH, SEQ, D_MODEL), dtype=f32),
        "x_sample": jax.random.normal(ks[1], (DEC_BATCH, DEC_SEQ, D_MODEL), dtype=f32),
        "norm_mix": gain(ks[2], (L, D_MODEL)),
        "w_in": nrm(ks[3], (L, D_MODEL, IN_COLS), D_MODEL ** -0.5),
        "sgu_norm": gain(ks[4], (L, A_WIDTH)),
        "w_spatial": nrm(ks[5], (L, A_HEADS, CHUNK, CHUNK), CHUNK ** -0.5),
        "b_spatial": 1.0 + 0.1 * jax.random.normal(ks[6], (L, A_HEADS, CHUNK), dtype=f32),
        "q_norm": gain(ks[7], (L, Q_RANK)),
        "w_uq": nrm(ks[8], (L, Q_RANK, B_HEADS * QK_DIM), Q_RANK ** -0.5),
        "kv_norm": gain(ks[9], (L, KV_RANK)),
        "w_ukv": nrm(ks[10], (L, KV_RANK, B_HEADS * (NOPE_DIM + V_DIM)), KV_RANK ** -0.5),
        "out_norm_a": gain(ks[11], (L, A_WIDTH)),
        "out_norm_b": gain(ks[12], (L, B_WIDTH)),
        "w_out": nrm(ks[13], (L, MIX_WIDTH, D_MODEL), MIX_WIDTH ** -0.5),
        "norm_ffn": gain(ks[14], (L, D_MODEL)),
        "w_ff1": nrm(ks[15], (L, D_MODEL, D_FF), D_MODEL ** -0.5),
        "w_ff2": nrm(ks[16], (L, D_FF, D_MODEL), D_FF ** -0.5),
        "norm_final": gain(ks[17], (D_MODEL,)),
    }


def reference(x_prompt, x_sample, norm_mix, w_in, sgu_norm, w_spatial, b_spatial, q_norm, w_uq,
              kv_norm, w_ukv, out_norm_a, out_norm_b, w_out, norm_ffn, w_ff1, w_ff2, norm_final):
    y_prompt = run_trunk(x_prompt, norm_mix, w_in, sgu_norm, w_spatial, b_spatial, q_norm, w_uq,
                         kv_norm, w_ukv, out_norm_a, out_norm_b, w_out, norm_ffn, w_ff1, w_ff2, norm_final)
    y_sample = run_trunk(x_sample, norm_mix, w_in, sgu_norm, w_spatial, b_spatial, q_norm, w_uq,
                         kv_norm, w_ukv, out_norm_a, out_norm_b, w_out, norm_ffn, w_ff1, w_ff2, norm_final)
    return (y_prompt, y_sample)
```

```python
import functools
import math

import jax
import jax.numpy as jnp
from jax.experimental import pallas as pl
from jax.experimental.pallas import tpu as pltpu

LANES = 128
D_MODEL = 1024
A_WIDTH = 512
A_HEADS = 4
A_HEAD_DIM = A_WIDTH // A_HEADS
CHUNK = 128
B_HEADS = 4
NOPE_DIM = 128
ROPE_DIM = 64
V_DIM = 128
QK_DIM = NOPE_DIM + ROPE_DIM
QK_PAD = 2 * LANES
B_WIDTH = B_HEADS * V_DIM
Q_RANK = 384
KV_RANK = 256
IN_COLS = 2 * A_WIDTH + Q_RANK + KV_RANK + ROPE_DIM
IN_COLS_PAD = 2 * A_WIDTH + Q_RANK + KV_RANK + LANES
D_FF = 4 * D_MODEL
ROPE_BASE = 10000.0
EPS = 1e-6
ATTN_SCALE = QK_DIM ** -0.5
VMEM_LIMIT_BYTES = 56 * 1024 * 1024

TOKEN_TILE = 512
Q_TILE = 512
K_TILE = 512
FF_CHUNK = 1024

_BF16 = jnp.bfloat16
_F32 = jnp.float32


def _rms(x, gain):
    return x * jax.lax.rsqrt(jnp.mean(x * x, axis=-1, keepdims=True) + EPS) * gain


def _gelu_tanh(x):
    c = math.sqrt(2.0 / math.pi)
    return 0.5 * x * (1.0 + jnp.tanh(c * (x + 0.044715 * (x * x * x))))


def _rope_padded(x, cos_t, sin_lo, sin_hi):
    half = ROPE_DIM // 2
    return (x * cos_t + pltpu.roll(x, LANES - half, 1) * sin_lo
            + pltpu.roll(x, half, 1) * sin_hi)


def _mix_in_kernel(x_ref, cos_ref, slo_ref, shi_ref, g_mix_ref, w_in_ref, g_sgu_ref,
                   w_sp_ref, b_sp_ref, g_q_ref, w_uq_ref, g_kv_ref, w_ukv_ref, g_a_ref,
                   ya_ref, q_ref, k_ref, v_ref):
    rows = x_ref.shape[0]
    h = _rms(x_ref[...], g_mix_ref[...]).astype(_BF16)
    z = jnp.dot(h, w_in_ref[...], preferred_element_type=_F32)

    u = _gelu_tanh(z[:, :A_WIDTH])
    v = _gelu_tanh(z[:, A_WIDTH:2 * A_WIDTH])
    cos_t, sin_lo, sin_hi = cos_ref[...], slo_ref[...], shi_ref[...]

    ya_cols = []
    for hd in range(A_HEADS):
        cols = slice(hd * A_HEAD_DIM, (hd + 1) * A_HEAD_DIM)
        vn = _rms(v[:, cols], g_sgu_ref[:, cols]).astype(_BF16)
        bias = jnp.broadcast_to(b_sp_ref[:, hd:hd + 1], (CHUNK, A_HEAD_DIM))
        parts = []
        for c in range(rows // CHUNK):
            s = jnp.dot(w_sp_ref[hd], vn[c * CHUNK:(c + 1) * CHUNK],
                        preferred_element_type=_F32) + bias
            parts.append(s)
        ya_cols.append(u[:, cols] * jnp.concatenate(parts, axis=0))
    ya = jnp.concatenate(ya_cols, axis=1)
    ya_ref[...] = _rms(ya, g_a_ref[...]).astype(_BF16)

    o = 2 * A_WIDTH
    cq = _rms(z[:, o:o + Q_RANK], g_q_ref[...]).astype(_BF16)
    q = jnp.dot(cq, w_uq_ref[...], preferred_element_type=_F32)
    for hd in range(B_HEADS):
        base = hd * QK_PAD
        q_ref[:, base:base + LANES] = (q[:, base:base + LANES] * ATTN_SCALE).astype(_BF16)
        pe = _rope_padded(q[:, base + LANES:base + QK_PAD], cos_t, sin_lo, sin_hi)
        q_ref[:, base + LANES:base + QK_PAD] = (pe * ATTN_SCALE).astype(_BF16)

    o += Q_RANK
    ckv = _rms(z[:, o:o + KV_RANK], g_kv_ref[...]).astype(_BF16)
    kv = jnp.dot(ckv, w_ukv_ref[...], preferred_element_type=_F32)
    o += KV_RANK
    k_pe = _rope_padded(z[:, o:o + LANES], cos_t, sin_lo, sin_hi).astype(_BF16)
    for hd in range(B_HEADS):
        src = hd * (NOPE_DIM + V_DIM)
        k_ref[:, hd * QK_PAD:hd * QK_PAD + LANES] = kv[:, src:src + NOPE_DIM].astype(_BF16)
        k_ref[:, hd * QK_PAD + LANES:(hd + 1) * QK_PAD] = k_pe
        v_ref[:, hd * V_DIM:(hd + 1) * V_DIM] = (
            kv[:, src + NOPE_DIM:src + NOPE_DIM + V_DIM].astype(_BF16))


def _attend_kernel(q_ref, k_ref, v_ref, o_ref, m_ref, l_ref, acc_ref):
    seq = k_ref.shape[0]
    m_ref[...] = jnp.full(m_ref.shape, -jnp.inf, _F32)
    l_ref[...] = jnp.zeros(l_ref.shape, _F32)
    acc_ref[...] = jnp.zeros(acc_ref.shape, _F32)
    q = q_ref[...]

    def step(t, carry):
        ks = pl.ds(pl.multiple_of(t * K_TILE, K_TILE), K_TILE)
        s = jax.lax.dot_general(q, k_ref[ks, :], (((1,), (1,)), ((), ())),
                                preferred_element_type=_F32)
        m_old = m_ref[...]
        m_new = jnp.maximum(m_old, jnp.max(s, axis=-1, keepdims=True))
        alpha = jnp.exp(m_old - m_new)
        p = jnp.exp(s - m_new)
        l_ref[...] = alpha * l_ref[...] + jnp.sum(p, axis=-1, keepdims=True)
        acc_ref[...] = alpha * acc_ref[...] + jnp.dot(
            p.astype(_BF16), v_ref[ks, :], preferred_element_type=_F32)
        m_ref[...] = m_new
        return carry

    jax.lax.fori_loop(0, seq // K_TILE, step, 0)
    o_ref[...] = (acc_ref[...] / l_ref[...]).astype(o_ref.dtype)


def _ffn_out_kernel(x_ref, ya_ref, yb_ref, g_b_ref, w_out_ref, g_ffn_ref, w1_ref, w2_ref,
                    g_fin_ref, o_ref, h_ref):
    yb = _rms(yb_ref[...], g_b_ref[...]).astype(_BF16)
    x1 = (x_ref[...]
          + jnp.dot(ya_ref[...], w_out_ref[:A_WIDTH, :], preferred_element_type=_F32)
          + jnp.dot(yb, w_out_ref[A_WIDTH:, :], preferred_element_type=_F32))
    o_ref[...] = x1
    h_ref[...] = _rms(x1, g_ffn_ref[...]).astype(_BF16)

    def chunk(c, carry):
        f = jnp.maximum(jnp.dot(h_ref[...], w1_ref[c], preferred_element_type=_F32), 0.0)
        o_ref[...] += jnp.dot((f * f).astype(_BF16), w2_ref[c], preferred_element_type=_F32)
        return carry

    jax.lax.fori_loop(0, D_FF // FF_CHUNK, chunk, 0)
    o_ref[...] = _rms(o_ref[...], g_fin_ref[...])


def _const_spec(shape):
    nd = len(shape)
    return pl.BlockSpec(shape, lambda *_: (0,) * nd, pipeline_mode=pl.Buffered(1))


def _rope_tables(seq):
    inv = 1.0 / (ROPE_BASE ** (jnp.arange(0, ROPE_DIM, 2, dtype=_F32) / ROPE_DIM))
    ang = jnp.arange(seq, dtype=_F32)[:, None] * inv[None, :]
    cos, sin = jnp.cos(ang), jnp.sin(ang)
    zero = jnp.zeros_like(cos)
    pad = jnp.zeros((seq, LANES - ROPE_DIM), _F32)
    cos_t = jnp.concatenate([cos, cos, pad], axis=1)
    sin_lo = jnp.concatenate([-sin, zero, pad], axis=1)
    sin_hi = jnp.concatenate([zero, sin, pad], axis=1)
    return cos_t, sin_lo, sin_hi


def _trunk(x, p):
    bsz, seq, _ = x.shape
    n_tok = bsz * seq
    x2 = x.reshape(n_tok, D_MODEL)
    tiles_per_seq = seq // TOKEN_TILE
    cos_t, sin_lo, sin_hi = _rope_tables(seq)

    tok = lambda width: pl.BlockSpec((TOKEN_TILE, width), lambda i: (i, 0))
    rope = pl.BlockSpec((TOKEN_TILE, LANES), lambda i: (i % tiles_per_seq, 0))
    params = pltpu.CompilerParams(dimension_semantics=("arbitrary",),
                                  vmem_limit_bytes=VMEM_LIMIT_BYTES)

    ya, q, k, v = pl.pallas_call(
        _mix_in_kernel,
        grid=(n_tok // TOKEN_TILE,),
        in_specs=[tok(D_MODEL), rope, rope, rope,
                  _const_spec((1, D_MODEL)), _const_spec((D_MODEL, IN_COLS_PAD)),
                  _const_spec((1, A_WIDTH)), _const_spec((A_HEADS, CHUNK, CHUNK)),
                  _const_spec((CHUNK, A_HEADS)), _const_spec((1, Q_RANK)),
                  _const_spec((Q_RANK, B_HEADS * QK_PAD)), _const_spec((1, KV_RANK)),
                  _const_spec((KV_RANK, B_HEADS * (NOPE_DIM + V_DIM))),
                  _const_spec((1, A_WIDTH))],
        out_specs=[tok(A_WIDTH), tok(B_HEADS * QK_PAD), tok(B_HEADS * QK_PAD), tok(B_WIDTH)],
        out_shape=[jax.ShapeDtypeStruct((n_tok, A_WIDTH), _BF16),
                   jax.ShapeDtypeStruct((n_tok, B_HEADS * QK_PAD), _BF16),
                   jax.ShapeDtypeStruct((n_tok, B_HEADS * QK_PAD), _BF16),
                   jax.ShapeDtypeStruct((n_tok, B_WIDTH), _BF16)],
        compiler_params=params,
        name="mix_in",
    )(x2, cos_t, sin_lo, sin_hi, p["g_mix"], p["w_in"], p["g_sgu"], p["w_sp"], p["b_sp"],
      p["g_q"], p["w_uq"], p["g_kv"], p["w_ukv"], p["g_a"])

    q_tiles = seq // Q_TILE
    yb = pl.pallas_call(
        _attend_kernel,
        grid=(bsz, B_HEADS, q_tiles),
        in_specs=[pl.BlockSpec((Q_TILE, QK_PAD), lambda b, h, i: (b * q_tiles + i, h)),
                  pl.BlockSpec((seq, QK_PAD), lambda b, h, i: (b, h)),
                  pl.BlockSpec((seq, V_DIM), lambda b, h, i: (b, h))],
        out_specs=pl.BlockSpec((Q_TILE, V_DIM), lambda b, h, i: (b * q_tiles + i, h)),
        out_shape=jax.ShapeDtypeStruct((n_tok, B_WIDTH), _F32),
        scratch_shapes=[pltpu.VMEM((Q_TILE, 1), _F32), pltpu.VMEM((Q_TILE, 1), _F32),
                        pltpu.VMEM((Q_TILE, V_DIM), _F32)],
        compiler_params=pltpu.CompilerParams(
            dimension_semantics=("arbitrary", "arbitrary", "arbitrary"),
            vmem_limit_bytes=VMEM_LIMIT_BYTES),
        name="attend",
    )(q, k, v)

    out = pl.pallas_call(
        _ffn_out_kernel,
        grid=(n_tok // TOKEN_TILE,),
        in_specs=[tok(D_MODEL), tok(A_WIDTH), tok(B_WIDTH), _const_spec((1, B_WIDTH)),
                  _const_spec((D_MODEL, D_MODEL)), _const_spec((1, D_MODEL)),
                  _const_spec((D_FF // FF_CHUNK, D_MODEL, FF_CHUNK)),
                  _const_spec((D_FF // FF_CHUNK, FF_CHUNK, D_MODEL)),
                  _const_spec((1, D_MODEL))],
        out_specs=tok(D_MODEL),
        out_shape=jax.ShapeDtypeStruct((n_tok, D_MODEL), _F32),
        scratch_shapes=[pltpu.VMEM((TOKEN_TILE, D_MODEL), _BF16)],
        compiler_params=params,
        name="ffn_out",
    )(x2, ya, yb, p["g_b"], p["w_out"], p["g_ffn"], p["w_ff1"], p["w_ff2"], p["g_fin"])
    return out.reshape(bsz, seq, D_MODEL)


def _prepare_params(norm_mix, w_in, sgu_norm, w_spatial, b_spatial, q_norm, w_uq, kv_norm,
                    w_ukv, out_norm_a, out_norm_b, w_out, norm_ffn, w_ff1, w_ff2, norm_final):
    w_in_p = jnp.pad(w_in[0], ((0, 0), (0, IN_COLS_PAD - IN_COLS))).astype(_BF16)
    uq = w_uq[0].reshape(Q_RANK, B_HEADS, QK_DIM)
    uq = jnp.pad(uq, ((0, 0), (0, 0), (0, QK_PAD - QK_DIM)))
    row = lambda g: g.reshape(1, -1).astype(_F32)
    n_chunks = D_FF // FF_CHUNK
    w1 = w_ff1[0].astype(_BF16).reshape(D_MODEL, n_chunks, FF_CHUNK).transpose(1, 0, 2)
    w2 = w_ff2[0].astype(_BF16).reshape(n_chunks, FF_CHUNK, D_MODEL)
    return dict(
        g_mix=row(norm_mix[0]), w_in=w_in_p, g_sgu=row(sgu_norm[0]),
        w_sp=w_spatial[0].astype(_BF16), b_sp=b_spatial[0].T.astype(_F32),
        g_q=row(q_norm[0]), w_uq=uq.reshape(Q_RANK, B_HEADS * QK_PAD).astype(_BF16),
        g_kv=row(kv_norm[0]), w_ukv=w_ukv[0].astype(_BF16), g_a=row(out_norm_a[0]),
        g_b=row(out_norm_b[0]), w_out=w_out[0].astype(_BF16), g_ffn=row(norm_ffn[0]),
        w_ff1=w1, w_ff2=w2, g_fin=row(norm_final))


def kernel(x_prompt, x_sample, norm_mix, w_in, sgu_norm, w_spatial, b_spatial, q_norm, w_uq,
           kv_norm, w_ukv, out_norm_a, out_norm_b, w_out, norm_ffn, w_ff1, w_ff2, norm_final):
    assert norm_mix.shape[0] == 1, "single-layer trunk"
    p = _prepare_params(norm_mix, w_in, sgu_norm, w_spatial, b_spatial, q_norm, w_uq, kv_norm,
                        w_ukv, out_norm_a, out_norm_b, w_out, norm_ffn, w_ff1, w_ff2,
                        norm_final)
    return (_trunk(x_prompt, p), _trunk(x_sample, p))
```

```python
import math

import jax
import jax.numpy as jnp
from jax.experimental import pallas as pl
from jax.experimental.pallas import tpu as pltpu

LANES = 128
D_MODEL = 1024
A_WIDTH = 512
A_HEADS = 4
A_HEAD_DIM = A_WIDTH // A_HEADS
CHUNK = 128
B_HEADS = 4
NOPE_DIM = 128
ROPE_DIM = 64
V_DIM = 128
QK_DIM = NOPE_DIM + ROPE_DIM
QK_PAD = 2 * LANES
VT_ROWS = V_DIM + 16
B_WIDTH = B_HEADS * V_DIM
Q_RANK = 384
KV_RANK = 256
IN_COLS = 2 * A_WIDTH + Q_RANK + KV_RANK + ROPE_DIM
IN_COLS_PAD = 2 * A_WIDTH + Q_RANK + KV_RANK + LANES
D_FF = 4 * D_MODEL
ROPE_BASE = 10000.0
EPS = 1e-6
ATTN_SCALE = QK_DIM ** -0.5
Q_SCALE = ATTN_SCALE * math.log2(math.e)
VMEM_LIMIT_BYTES = 56 * 1024 * 1024

TOKEN_TILE = 512
Q_TILE = 512
K_TILE = 512
FF_CHUNK = 1024

_BF16 = jnp.bfloat16
_F32 = jnp.float32


def _rms(x, gain):
    return x * jax.lax.rsqrt(jnp.mean(x * x, axis=-1, keepdims=True) + EPS) * gain


def _gelu_tanh(x):
    c = math.sqrt(2.0 / math.pi)
    return 0.5 * x * (1.0 + jnp.tanh(c * (x + 0.044715 * (x * x * x))))


def _rope_padded(x, cos_t, sin_lo, sin_hi):
    half = ROPE_DIM // 2
    return (x * cos_t + pltpu.roll(x, LANES - half, 1) * sin_lo
            + pltpu.roll(x, half, 1) * sin_hi)


def _mix_in_kernel(x_ref, cos_ref, slo_ref, shi_ref, g_mix_ref, w_in_ref, g_sgu_ref,
                   w_sp_ref, b_sp_ref, g_q_ref, w_uq_ref, g_kv_ref, w_ukv_ref, g_a_ref,
                   ya_ref, q_ref, k_ref, vt_ref):
    rows = x_ref.shape[0]
    h = _rms(x_ref[...], g_mix_ref[...]).astype(_BF16)
    z = jnp.dot(h, w_in_ref[...], preferred_element_type=_F32)

    u = _gelu_tanh(z[:, :A_WIDTH])
    v = _gelu_tanh(z[:, A_WIDTH:2 * A_WIDTH])
    cos_t, sin_lo, sin_hi = cos_ref[...], slo_ref[...], shi_ref[...]

    ya_cols = []
    for hd in range(A_HEADS):
        cols = slice(hd * A_HEAD_DIM, (hd + 1) * A_HEAD_DIM)
        vn = _rms(v[:, cols], g_sgu_ref[:, cols]).astype(_BF16)
        bias = jnp.broadcast_to(b_sp_ref[:, hd:hd + 1], (CHUNK, A_HEAD_DIM))
        parts = []
        for c in range(rows // CHUNK):
            s = jnp.dot(w_sp_ref[hd], vn[c * CHUNK:(c + 1) * CHUNK],
                        preferred_element_type=_F32) + bias
            parts.append(s)
        ya_cols.append(u[:, cols] * jnp.concatenate(parts, axis=0))
    ya = jnp.concatenate(ya_cols, axis=1)
    ya_ref[...] = _rms(ya, g_a_ref[...]).astype(_BF16)

    o = 2 * A_WIDTH
    cq = _rms(z[:, o:o + Q_RANK], g_q_ref[...]).astype(_BF16)
    q = jnp.dot(cq, w_uq_ref[...], preferred_element_type=_F32)
    for hd in range(B_HEADS):
        base = hd * QK_PAD
        q_ref[:, base:base + LANES] = (q[:, base:base + LANES] * Q_SCALE).astype(_BF16)
        pe = _rope_padded(q[:, base + LANES:base + QK_PAD], cos_t, sin_lo, sin_hi)
        q_ref[:, base + LANES:base + QK_PAD] = (pe * Q_SCALE).astype(_BF16)

    o += Q_RANK
    ckv = _rms(z[:, o:o + KV_RANK], g_kv_ref[...]).astype(_BF16)
    kv = jnp.dot(ckv, w_ukv_ref[...], preferred_element_type=_F32)
    o += KV_RANK
    k_pe = _rope_padded(z[:, o:o + LANES], cos_t, sin_lo, sin_hi).astype(_BF16)
    for hd in range(B_HEADS):
        src = hd * (NOPE_DIM + V_DIM)
        k_ref[:, hd * QK_PAD:hd * QK_PAD + LANES] = kv[:, src:src + NOPE_DIM].astype(_BF16)
        k_ref[:, hd * QK_PAD + LANES:(hd + 1) * QK_PAD] = k_pe
        vt_ref[hd, :V_DIM, :] = kv[:, src + NOPE_DIM:src + NOPE_DIM + V_DIM].T.astype(_BF16)
        vt_ref[hd, V_DIM:, :] = jnp.ones((VT_ROWS - V_DIM, rows), _BF16)


def _attend_kernel(q_ref, k_ref, vt_ref, o_ref):
    n_tiles = k_ref.shape[0] // K_TILE
    q = q_ref[...]

    def scores_t(t):
        return jax.lax.dot_general(k_ref[t * K_TILE:(t + 1) * K_TILE, :], q,
                                   (((1,), (1,)), ((), ())),
                                   preferred_element_type=_F32)

    m = jnp.full((1, Q_TILE), -jnp.inf, _F32)
    acc = jnp.zeros((VT_ROWS, Q_TILE), _F32)
    st_next = scores_t(0)
    for t in range(n_tiles):
        st = st_next
        if t + 1 < n_tiles:
            st_next = scores_t(t + 1)
        m_new = jnp.maximum(m, jnp.max(st, axis=0, keepdims=True))
        alpha = jnp.exp2(m - m_new)
        p = jnp.exp2(st - m_new).astype(_BF16)
        acc = alpha * acc + jnp.dot(vt_ref[:, t * K_TILE:(t + 1) * K_TILE], p,
                                    preferred_element_type=_F32)
        m = m_new
    o_ref[...] = (acc[:V_DIM] / acc[V_DIM:V_DIM + 1]).T.astype(o_ref.dtype)


def _ffn_out_kernel(x_ref, ya_ref, yb_ref, g_b_ref, w_out_ref, g_ffn_ref, w1_ref, w2_ref,
                    g_fin_ref, o_ref, h_ref):
    yb = _rms(yb_ref[...], g_b_ref[...]).astype(_BF16)
    x1 = (x_ref[...]
          + jnp.dot(ya_ref[...], w_out_ref[:A_WIDTH, :], preferred_element_type=_F32)
          + jnp.dot(yb, w_out_ref[A_WIDTH:, :], preferred_element_type=_F32))
    o_ref[...] = x1
    h_ref[...] = _rms(x1, g_ffn_ref[...]).astype(_BF16)

    def chunk(c, carry):
        f = jnp.maximum(jnp.dot(h_ref[...], w1_ref[c], preferred_element_type=_F32), 0.0)
        o_ref[...] += jnp.dot((f * f).astype(_BF16), w2_ref[c], preferred_element_type=_F32)
        return carry

    jax.lax.fori_loop(0, D_FF // FF_CHUNK, chunk, 0)
    o_ref[...] = _rms(o_ref[...], g_fin_ref[...])


def _const_spec(shape):
    nd = len(shape)
    return pl.BlockSpec(shape, lambda *_: (0,) * nd, pipeline_mode=pl.Buffered(1))


def _rope_tables(seq):
    inv = 1.0 / (ROPE_BASE ** (jnp.arange(0, ROPE_DIM, 2, dtype=_F32) / ROPE_DIM))
    ang = jnp.arange(seq, dtype=_F32)[:, None] * inv[None, :]
    cos, sin = jnp.cos(ang), jnp.sin(ang)
    zero = jnp.zeros_like(cos)
    pad = jnp.zeros((seq, LANES - ROPE_DIM), _F32)
    cos_t = jnp.concatenate([cos, cos, pad], axis=1)
    sin_lo = jnp.concatenate([-sin, zero, pad], axis=1)
    sin_hi = jnp.concatenate([zero, sin, pad], axis=1)
    return cos_t, sin_lo, sin_hi


def _trunk(x, p):
    bsz, seq, _ = x.shape
    n_tok = bsz * seq
    x2 = x.reshape(n_tok, D_MODEL)
    tiles_per_seq = seq // TOKEN_TILE
    cos_t, sin_lo, sin_hi = _rope_tables(seq)

    tok = lambda width: pl.BlockSpec((TOKEN_TILE, width), lambda i: (i, 0))
    rope = pl.BlockSpec((TOKEN_TILE, LANES), lambda i: (i % tiles_per_seq, 0))
    params = pltpu.CompilerParams(dimension_semantics=("arbitrary",),
                                  vmem_limit_bytes=VMEM_LIMIT_BYTES)

    ya, q, k, vt = pl.pallas_call(
        _mix_in_kernel,
        grid=(n_tok // TOKEN_TILE,),
        in_specs=[tok(D_MODEL), rope, rope, rope,
                  _const_spec((1, D_MODEL)), _const_spec((D_MODEL, IN_COLS_PAD)),
                  _const_spec((1, A_WIDTH)), _const_spec((A_HEADS, CHUNK, CHUNK)),
                  _const_spec((CHUNK, A_HEADS)), _const_spec((1, Q_RANK)),
                  _const_spec((Q_RANK, B_HEADS * QK_PAD)), _const_spec((1, KV_RANK)),
                  _const_spec((KV_RANK, B_HEADS * (NOPE_DIM + V_DIM))),
                  _const_spec((1, A_WIDTH))],
        out_specs=[tok(A_WIDTH), tok(B_HEADS * QK_PAD), tok(B_HEADS * QK_PAD),
                   pl.BlockSpec((None, B_HEADS, VT_ROWS, TOKEN_TILE),
                                lambda i: (i // tiles_per_seq, 0, 0, i % tiles_per_seq))],
        out_shape=[jax.ShapeDtypeStruct((n_tok, A_WIDTH), _BF16),
                   jax.ShapeDtypeStruct((n_tok, B_HEADS * QK_PAD), _BF16),
                   jax.ShapeDtypeStruct((n_tok, B_HEADS * QK_PAD), _BF16),
                   jax.ShapeDtypeStruct((bsz, B_HEADS, VT_ROWS, seq), _BF16)],
        compiler_params=params,
        name="mix_in",
    )(x2, cos_t, sin_lo, sin_hi, p["g_mix"], p["w_in"], p["g_sgu"], p["w_sp"], p["b_sp"],
      p["g_q"], p["w_uq"], p["g_kv"], p["w_ukv"], p["g_a"])

    q_tiles = seq // Q_TILE
    yb = pl.pallas_call(
        _attend_kernel,
        grid=(bsz, B_HEADS, q_tiles),
        in_specs=[pl.BlockSpec((Q_TILE, QK_PAD), lambda b, h, i: (b * q_tiles + i, h)),
                  pl.BlockSpec((seq, QK_PAD), lambda b, h, i: (b, h)),
                  pl.BlockSpec((None, None, VT_ROWS, seq), lambda b, h, i: (b, h, 0, 0))],
        out_specs=pl.BlockSpec((Q_TILE, V_DIM), lambda b, h, i: (b * q_tiles + i, h)),
        out_shape=jax.ShapeDtypeStruct((n_tok, B_WIDTH), _F32),
        compiler_params=pltpu.CompilerParams(
            dimension_semantics=("arbitrary", "arbitrary", "arbitrary"),
            vmem_limit_bytes=VMEM_LIMIT_BYTES),
        name="attend",
    )(q, k, vt)

    out = pl.pallas_call(
        _ffn_out_kernel,
        grid=(n_tok // TOKEN_TILE,),
        in_specs=[tok(D_MODEL), tok(A_WIDTH), tok(B_WIDTH), _const_spec((1, B_WIDTH)),
                  _const_spec((D_MODEL, D_MODEL)), _const_spec((1, D_MODEL)),
                  _const_spec((D_FF // FF_CHUNK, D_MODEL, FF_CHUNK)),
                  _const_spec((D_FF // FF_CHUNK, FF_CHUNK, D_MODEL)),
                  _const_spec((1, D_MODEL))],
        out_specs=tok(D_MODEL),
        out_shape=jax.ShapeDtypeStruct((n_tok, D_MODEL), _F32),
        scratch_shapes=[pltpu.VMEM((TOKEN_TILE, D_MODEL), _BF16)],
        compiler_params=params,
        name="ffn_out",
    )(x2, ya, yb, p["g_b"], p["w_out"], p["g_ffn"], p["w_ff1"], p["w_ff2"], p["g_fin"])
    return out.reshape(bsz, seq, D_MODEL)


def _prepare_params(norm_mix, w_in, sgu_norm, w_spatial, b_spatial, q_norm, w_uq, kv_norm,
                    w_ukv, out_norm_a, out_norm_b, w_out, norm_ffn, w_ff1, w_ff2, norm_final):
    w_in_p = jnp.pad(w_in[0], ((0, 0), (0, IN_COLS_PAD - IN_COLS))).astype(_BF16)
    uq = w_uq[0].reshape(Q_RANK, B_HEADS, QK_DIM)
    uq = jnp.pad(uq, ((0, 0), (0, 0), (0, QK_PAD - QK_DIM)))
    row = lambda g: g.reshape(1, -1).astype(_F32)
    n_chunks = D_FF // FF_CHUNK
    w1 = w_ff1[0].astype(_BF16).reshape(D_MODEL, n_chunks, FF_CHUNK).transpose(1, 0, 2)
    w2 = w_ff2[0].astype(_BF16).reshape(n_chunks, FF_CHUNK, D_MODEL)
    return dict(
        g_mix=row(norm_mix[0]), w_in=w_in_p, g_sgu=row(sgu_norm[0]),
        w_sp=w_spatial[0].astype(_BF16), b_sp=b_spatial[0].T.astype(_F32),
        g_q=row(q_norm[0]), w_uq=uq.reshape(Q_RANK, B_HEADS * QK_PAD).astype(_BF16),
        g_kv=row(kv_norm[0]), w_ukv=w_ukv[0].astype(_BF16), g_a=row(out_norm_a[0]),
        g_b=row(out_norm_b[0]), w_out=w_out[0].astype(_BF16), g_ffn=row(norm_ffn[0]),
        w_ff1=w1, w_ff2=w2, g_fin=row(norm_final))


def kernel(x_prompt, x_sample, norm_mix, w_in, sgu_norm, w_spatial, b_spatial, q_norm, w_uq,
           kv_norm, w_ukv, out_norm_a, out_norm_b, w_out, norm_ffn, w_ff1, w_ff2, norm_final):
    assert norm_mix.shape[0] == 1, "single-layer trunk"
    p = _prepare_params(norm_mix, w_in, sgu_norm, w_spatial, b_spatial, q_norm, w_uq, kv_norm,
                        w_ukv, out_norm_a, out_norm_b, w_out, norm_ffn, w_ff1, w_ff2,
                        norm_final)
    return (_trunk(x_prompt, p), _trunk(x_sample, p))
```

```python
import math

import jax
import jax.numpy as jnp
from jax.experimental import pallas as pl
from jax.experimental.pallas import tpu as pltpu

LANES = 128
D_MODEL = 1024
A_WIDTH = 512
A_HEADS = 4
A_HEAD_DIM = A_WIDTH // A_HEADS
CHUNK = 128
B_HEADS = 4
NOPE_DIM = 128
ROPE_DIM = 64
V_DIM = 128
QK_DIM = NOPE_DIM + ROPE_DIM
QK_PAD = 2 * LANES
VT_ROWS = V_DIM + 16
B_WIDTH = B_HEADS * V_DIM
Q_RANK = 384
KV_RANK = 256
IN_COLS = 2 * A_WIDTH + Q_RANK + KV_RANK + ROPE_DIM
IN_COLS_PAD = 2 * A_WIDTH + Q_RANK + KV_RANK + LANES
D_FF = 4 * D_MODEL
ROPE_BASE = 10000.0
EPS = 1e-6
ATTN_SCALE = QK_DIM ** -0.5
Q_SCALE = ATTN_SCALE * math.log2(math.e)
VMEM_LIMIT_BYTES = 56 * 1024 * 1024

TOKEN_TILE = 512
Q_TILE = 512
K_TILE = 512
K_AHEAD = 2
FF_CHUNK = 1024

_BF16 = jnp.bfloat16
_F32 = jnp.float32


def _rms(x, gain):
    return x * jax.lax.rsqrt(jnp.mean(x * x, axis=-1, keepdims=True) + EPS) * gain


def _gelu_tanh(x):
    c = math.sqrt(2.0 / math.pi)
    return 0.5 * x * (1.0 + jnp.tanh(c * (x + 0.044715 * (x * x * x))))


def _rope_padded(x, cos_t, sin_lo, sin_hi):
    half = ROPE_DIM // 2
    return (x * cos_t + pltpu.roll(x, LANES - half, 1) * sin_lo
            + pltpu.roll(x, half, 1) * sin_hi)


def _mix_in_kernel(x_ref, cos_ref, slo_ref, shi_ref, g_mix_ref, w_in_ref, g_sgu_ref,
                   w_sp_ref, b_sp_ref, g_q_ref, w_uq_ref, g_kv_ref, w_ukv_ref, g_a_ref,
                   ya_ref, q_ref, k_ref, vt_ref):
    rows = x_ref.shape[0]
    h = _rms(x_ref[...], g_mix_ref[...]).astype(_BF16)
    z = jnp.dot(h, w_in_ref[...], preferred_element_type=_F32)

    u = _gelu_tanh(z[:, :A_WIDTH])
    v = _gelu_tanh(z[:, A_WIDTH:2 * A_WIDTH])
    cos_t, sin_lo, sin_hi = cos_ref[...], slo_ref[...], shi_ref[...]

    ya_cols = []
    for hd in range(A_HEADS):
        cols = slice(hd * A_HEAD_DIM, (hd + 1) * A_HEAD_DIM)
        vn = _rms(v[:, cols], g_sgu_ref[:, cols]).astype(_BF16)
        bias = jnp.broadcast_to(b_sp_ref[:, hd:hd + 1], (CHUNK, A_HEAD_DIM))
        parts = []
        for c in range(rows // CHUNK):
            s = jnp.dot(w_sp_ref[hd], vn[c * CHUNK:(c + 1) * CHUNK],
                        preferred_element_type=_F32) + bias
            parts.append(s)
        ya_cols.append(u[:, cols] * jnp.concatenate(parts, axis=0))
    ya = jnp.concatenate(ya_cols, axis=1)
    ya_ref[...] = _rms(ya, g_a_ref[...]).astype(_BF16)

    o = 2 * A_WIDTH
    cq = _rms(z[:, o:o + Q_RANK], g_q_ref[...]).astype(_BF16)
    q = jnp.dot(cq, w_uq_ref[...], preferred_element_type=_F32)
    for hd in range(B_HEADS):
        base = hd * QK_PAD
        q_ref[:, base:base + LANES] = (q[:, base:base + LANES] * Q_SCALE).astype(_BF16)
        pe = _rope_padded(q[:, base + LANES:base + QK_PAD], cos_t, sin_lo, sin_hi)
        q_ref[:, base + LANES:base + QK_PAD] = (pe * Q_SCALE).astype(_BF16)

    o += Q_RANK
    ckv = _rms(z[:, o:o + KV_RANK], g_kv_ref[...]).astype(_BF16)
    kv = jnp.dot(ckv, w_ukv_ref[...], preferred_element_type=_F32)
    o += KV_RANK
    k_pe = _rope_padded(z[:, o:o + LANES], cos_t, sin_lo, sin_hi).astype(_BF16)
    for hd in range(B_HEADS):
        src = hd * (NOPE_DIM + V_DIM)
        k_ref[:, hd * QK_PAD:hd * QK_PAD + LANES] = kv[:, src:src + NOPE_DIM].astype(_BF16)
        k_ref[:, hd * QK_PAD + LANES:(hd + 1) * QK_PAD] = k_pe
        vt_ref[hd, :V_DIM, :] = kv[:, src + NOPE_DIM:src + NOPE_DIM + V_DIM].T.astype(_BF16)
        vt_ref[hd, V_DIM:, :] = jnp.ones((VT_ROWS - V_DIM, rows), _BF16)


def _attend_kernel(q_ref, k_ref, vt_ref, o_ref):
    n_tiles = k_ref.shape[0] // K_TILE
    tiles = [(hd, t) for hd in range(B_HEADS) for t in range(n_tiles)]

    def scores_t(hd, t):
        cols = slice(hd * QK_PAD, (hd + 1) * QK_PAD)
        return jax.lax.dot_general(k_ref[t * K_TILE:(t + 1) * K_TILE, cols], q_ref[:, cols],
                                   (((1,), (1,)), ((), ())),
                                   preferred_element_type=_F32)

    ahead = [scores_t(*tl) for tl in tiles[:K_AHEAD]]
    for i, (hd, t) in enumerate(tiles):
        st = ahead.pop(0)
        if i + K_AHEAD < len(tiles):
            ahead.append(scores_t(*tiles[i + K_AHEAD]))
        if t == 0:
            m = jnp.full((1, Q_TILE), -jnp.inf, _F32)
            acc = jnp.zeros((VT_ROWS, Q_TILE), _F32)
        m_new = jnp.maximum(m, jnp.max(st, axis=0, keepdims=True))
        alpha = jnp.exp2(m - m_new)
        p = jnp.exp2(st - m_new).astype(_BF16)
        acc = alpha * acc + jnp.dot(vt_ref[hd, :, t * K_TILE:(t + 1) * K_TILE], p,
                                    preferred_element_type=_F32)
        m = m_new
        if t == n_tiles - 1:
            o_ref[:, hd * V_DIM:(hd + 1) * V_DIM] = (
                acc[:V_DIM] / acc[V_DIM:V_DIM + 1]).T.astype(o_ref.dtype)


def _ffn_out_kernel(x_ref, ya_ref, yb_ref, g_b_ref, w_out_ref, g_ffn_ref, w1_ref, w2_ref,
                    g_fin_ref, o_ref, h_ref):
    yb = _rms(yb_ref[...], g_b_ref[...]).astype(_BF16)
    x1 = (x_ref[...]
          + jnp.dot(ya_ref[...], w_out_ref[:A_WIDTH, :], preferred_element_type=_F32)
          + jnp.dot(yb, w_out_ref[A_WIDTH:, :], preferred_element_type=_F32))
    o_ref[...] = x1
    h_ref[...] = _rms(x1, g_ffn_ref[...]).astype(_BF16)

    def chunk(c, carry):
        f = jnp.maximum(jnp.dot(h_ref[...], w1_ref[c], preferred_element_type=_F32), 0.0)
        o_ref[...] += jnp.dot((f * f).astype(_BF16), w2_ref[c], preferred_element_type=_F32)
        return carry

    jax.lax.fori_loop(0, D_FF // FF_CHUNK, chunk, 0)
    o_ref[...] = _rms(o_ref[...], g_fin_ref[...])


def _const_spec(shape):
    nd = len(shape)
    return pl.BlockSpec(shape, lambda *_: (0,) * nd, pipeline_mode=pl.Buffered(1))


def _rope_tables(seq):
    inv = 1.0 / (ROPE_BASE ** (jnp.arange(0, ROPE_DIM, 2, dtype=_F32) / ROPE_DIM))
    ang = jnp.arange(seq, dtype=_F32)[:, None] * inv[None, :]
    cos, sin = jnp.cos(ang), jnp.sin(ang)
    zero = jnp.zeros_like(cos)
    pad = jnp.zeros((seq, LANES - ROPE_DIM), _F32)
    cos_t = jnp.concatenate([cos, cos, pad], axis=1)
    sin_lo = jnp.concatenate([-sin, zero, pad], axis=1)
    sin_hi = jnp.concatenate([zero, sin, pad], axis=1)
    return cos_t, sin_lo, sin_hi


def _trunk(x, p):
    bsz, seq, _ = x.shape
    n_tok = bsz * seq
    x2 = x.reshape(n_tok, D_MODEL)
    tiles_per_seq = seq // TOKEN_TILE
    cos_t, sin_lo, sin_hi = _rope_tables(seq)

    tok = lambda width: pl.BlockSpec((TOKEN_TILE, width), lambda i: (i, 0))
    rope = pl.BlockSpec((TOKEN_TILE, LANES), lambda i: (i % tiles_per_seq, 0))
    params = pltpu.CompilerParams(dimension_semantics=("arbitrary",),
                                  vmem_limit_bytes=VMEM_LIMIT_BYTES)

    ya, q, k, vt = pl.pallas_call(
        _mix_in_kernel,
        grid=(n_tok // TOKEN_TILE,),
        in_specs=[tok(D_MODEL), rope, rope, rope,
                  _const_spec((1, D_MODEL)), _const_spec((D_MODEL, IN_COLS_PAD)),
                  _const_spec((1, A_WIDTH)), _const_spec((A_HEADS, CHUNK, CHUNK)),
                  _const_spec((CHUNK, A_HEADS)), _const_spec((1, Q_RANK)),
                  _const_spec((Q_RANK, B_HEADS * QK_PAD)), _const_spec((1, KV_RANK)),
                  _const_spec((KV_RANK, B_HEADS * (NOPE_DIM + V_DIM))),
                  _const_spec((1, A_WIDTH))],
        out_specs=[tok(A_WIDTH), tok(B_HEADS * QK_PAD), tok(B_HEADS * QK_PAD),
                   pl.BlockSpec((None, B_HEADS, VT_ROWS, TOKEN_TILE),
                                lambda i: (i // tiles_per_seq, 0, 0, i % tiles_per_seq))],
        out_shape=[jax.ShapeDtypeStruct((n_tok, A_WIDTH), _BF16),
                   jax.ShapeDtypeStruct((n_tok, B_HEADS * QK_PAD), _BF16),
                   jax.ShapeDtypeStruct((n_tok, B_HEADS * QK_PAD), _BF16),
                   jax.ShapeDtypeStruct((bsz, B_HEADS, VT_ROWS, seq), _BF16)],
        compiler_params=params,
        name="mix_in",
    )(x2, cos_t, sin_lo, sin_hi, p["g_mix"], p["w_in"], p["g_sgu"], p["w_sp"], p["b_sp"],
      p["g_q"], p["w_uq"], p["g_kv"], p["w_ukv"], p["g_a"])

    q_tiles = seq // Q_TILE
    yb = pl.pallas_call(
        _attend_kernel,
        grid=(bsz, q_tiles),
        in_specs=[pl.BlockSpec((Q_TILE, B_HEADS * QK_PAD), lambda b, i: (b * q_tiles + i, 0)),
                  pl.BlockSpec((seq, B_HEADS * QK_PAD), lambda b, i: (b, 0)),
                  pl.BlockSpec((None, B_HEADS, VT_ROWS, seq), lambda b, i: (b, 0, 0, 0))],
        out_specs=pl.BlockSpec((Q_TILE, B_WIDTH), lambda b, i: (b * q_tiles + i, 0)),
        out_shape=jax.ShapeDtypeStruct((n_tok, B_WIDTH), _F32),
        compiler_params=pltpu.CompilerParams(
            dimension_semantics=("arbitrary", "arbitrary"),
            vmem_limit_bytes=VMEM_LIMIT_BYTES),
        name="attend",
    )(q, k, vt)

    out = pl.pallas_call(
        _ffn_out_kernel,
        grid=(n_tok // TOKEN_TILE,),
        in_specs=[tok(D_MODEL), tok(A_WIDTH), tok(B_WIDTH), _const_spec((1, B_WIDTH)),
                  _const_spec((D_MODEL, D_MODEL)), _const_spec((1, D_MODEL)),
                  _const_spec((D_FF // FF_CHUNK, D_MODEL, FF_CHUNK)),
                  _const_spec((D_FF // FF_CHUNK, FF_CHUNK, D_MODEL)),
                  _const_spec((1, D_MODEL))],
        out_specs=tok(D_MODEL),
        out_shape=jax.ShapeDtypeStruct((n_tok, D_MODEL), _F32),
        scratch_shapes=[pltpu.VMEM((TOKEN_TILE, D_MODEL), _BF16)],
        compiler_params=params,
        name="ffn_out",
    )(x2, ya, yb, p["g_b"], p["w_out"], p["g_ffn"], p["w_ff1"], p["w_ff2"], p["g_fin"])
    return out.reshape(bsz, seq, D_MODEL)


def _prepare_params(norm_mix, w_in, sgu_norm, w_spatial, b_spatial, q_norm, w_uq, kv_norm,
                    w_ukv, out_norm_a, out_norm_b, w_out, norm_ffn, w_ff1, w_ff2, norm_final):
    w_in_p = jnp.pad(w_in[0], ((0, 0), (0, IN_COLS_PAD - IN_COLS))).astype(_BF16)
    uq = w_uq[0].reshape(Q_RANK, B_HEADS, QK_DIM)
    uq = jnp.pad(uq, ((0, 0), (0, 0), (0, QK_PAD - QK_DIM)))
    row = lambda g: g.reshape(1, -1).astype(_F32)
    n_chunks = D_FF // FF_CHUNK
    w1 = w_ff1[0].astype(_BF16).reshape(D_MODEL, n_chunks, FF_CHUNK).transpose(1, 0, 2)
    w2 = w_ff2[0].astype(_BF16).reshape(n_chunks, FF_CHUNK, D_MODEL)
    return dict(
        g_mix=row(norm_mix[0]), w_in=w_in_p, g_sgu=row(sgu_norm[0]),
        w_sp=w_spatial[0].astype(_BF16), b_sp=b_spatial[0].T.astype(_F32),
        g_q=row(q_norm[0]), w_uq=uq.reshape(Q_RANK, B_HEADS * QK_PAD).astype(_BF16),
        g_kv=row(kv_norm[0]), w_ukv=w_ukv[0].astype(_BF16), g_a=row(out_norm_a[0]),
        g_b=row(out_norm_b[0]), w_out=w_out[0].astype(_BF16), g_ffn=row(norm_ffn[0]),
        w_ff1=w1, w_ff2=w2, g_fin=row(norm_final))


def kernel(x_prompt, x_sample, norm_mix, w_in, sgu_norm, w_spatial, b_spatial, q_norm, w_uq,
           kv_norm, w_ukv, out_norm_a, out_norm_b, w_out, norm_ffn, w_ff1, w_ff2, norm_final):
    assert norm_mix.shape[0] == 1, "single-layer trunk"
    p = _prepare_params(norm_mix, w_in, sgu_norm, w_spatial, b_spatial, q_norm, w_uq, kv_norm,
                        w_ukv, out_norm_a, out_norm_b, w_out, norm_ffn, w_ff1, w_ff2,
                        norm_final)
    return (_trunk(x_prompt, p), _trunk(x_sample, p))
```

```python
import math

import jax
import jax.numpy as jnp
from jax.experimental import pallas as pl
from jax.experimental.pallas import tpu as pltpu

LANES = 128
D_MODEL = 1024
A_WIDTH = 512
A_HEADS = 4
A_HEAD_DIM = A_WIDTH // A_HEADS
CHUNK = 128
B_HEADS = 4
NOPE_DIM = 128
ROPE_DIM = 64
V_DIM = 128
QK_DIM = NOPE_DIM + ROPE_DIM
QK_PAD = 2 * LANES
VT_ROWS = V_DIM + 16
B_WIDTH = B_HEADS * V_DIM
Q_RANK = 384
KV_RANK = 256
IN_COLS = 2 * A_WIDTH + Q_RANK + KV_RANK + ROPE_DIM
IN_COLS_PAD = 2 * A_WIDTH + Q_RANK + KV_RANK + LANES
D_FF = 4 * D_MODEL
ROPE_BASE = 10000.0
EPS = 1e-6
ATTN_SCALE = QK_DIM ** -0.5
Q_SCALE = ATTN_SCALE * math.log2(math.e)
VMEM_LIMIT_BYTES = 56 * 1024 * 1024

TOKEN_TILE = 512
Q_TILE = 512
K_TILE = 512
K_AHEAD = 1
LAG_LIMIT = 64.0
assert K_TILE == TOKEN_TILE
FF_CHUNK = 1024

_BF16 = jnp.bfloat16
_F32 = jnp.float32


def _rms(x, gain):
    return x * jax.lax.rsqrt(jnp.mean(x * x, axis=-1, keepdims=True) + EPS) * gain


def _gelu_tanh(x):
    c = math.sqrt(2.0 / math.pi)
    return 0.5 * x * (1.0 + jnp.tanh(c * (x + 0.044715 * (x * x * x))))


def _rope_padded(x, cos_t, sin_lo, sin_hi):
    half = ROPE_DIM // 2
    return (x * cos_t + pltpu.roll(x, LANES - half, 1) * sin_lo
            + pltpu.roll(x, half, 1) * sin_hi)


def _mix_in_kernel(x_ref, cos_ref, slo_ref, shi_ref, g_mix_ref, w_in_ref, g_sgu_ref,
                   w_sp_ref, b_sp_ref, g_q_ref, w_uq_ref, g_kv_ref, w_ukv_ref, g_a_ref,
                   ya_ref, q_ref, k_ref, vt_ref):
    rows = x_ref.shape[0]
    h = _rms(x_ref[...], g_mix_ref[...]).astype(_BF16)
    z = jnp.dot(h, w_in_ref[...], preferred_element_type=_F32)

    u = _gelu_tanh(z[:, :A_WIDTH])
    v = _gelu_tanh(z[:, A_WIDTH:2 * A_WIDTH])
    cos_t, sin_lo, sin_hi = cos_ref[...], slo_ref[...], shi_ref[...]

    ya_cols = []
    for hd in range(A_HEADS):
        cols = slice(hd * A_HEAD_DIM, (hd + 1) * A_HEAD_DIM)
        vn = _rms(v[:, cols], g_sgu_ref[:, cols]).astype(_BF16)
        bias = jnp.broadcast_to(b_sp_ref[:, hd:hd + 1], (CHUNK, A_HEAD_DIM))
        parts = []
        for c in range(rows // CHUNK):
            s = jnp.dot(w_sp_ref[hd], vn[c * CHUNK:(c + 1) * CHUNK],
                        preferred_element_type=_F32) + bias
            parts.append(s)
        ya_cols.append(u[:, cols] * jnp.concatenate(parts, axis=0))
    ya = jnp.concatenate(ya_cols, axis=1)
    ya_ref[...] = _rms(ya, g_a_ref[...]).astype(_BF16)

    o = 2 * A_WIDTH
    cq = _rms(z[:, o:o + Q_RANK], g_q_ref[...]).astype(_BF16)
    q = jnp.dot(cq, w_uq_ref[...], preferred_element_type=_F32)
    for hd in range(B_HEADS):
        base = hd * QK_PAD
        q_ref[:, base:base + LANES] = (q[:, base:base + LANES] * Q_SCALE).astype(_BF16)
        pe = _rope_padded(q[:, base + LANES:base + QK_PAD], cos_t, sin_lo, sin_hi)
        q_ref[:, base + LANES:base + QK_PAD] = (pe * Q_SCALE).astype(_BF16)

    o += Q_RANK
    ckv = _rms(z[:, o:o + KV_RANK], g_kv_ref[...]).astype(_BF16)
    kv = jnp.dot(ckv, w_ukv_ref[...], preferred_element_type=_F32)
    o += KV_RANK
    k_pe = _rope_padded(z[:, o:o + LANES], cos_t, sin_lo, sin_hi).astype(_BF16)
    for hd in range(B_HEADS):
        src = hd * (NOPE_DIM + V_DIM)
        k_ref[:, hd * QK_PAD:hd * QK_PAD + LANES] = kv[:, src:src + NOPE_DIM].astype(_BF16)
        k_ref[:, hd * QK_PAD + LANES:(hd + 1) * QK_PAD] = k_pe
        vt_ref[hd, :V_DIM, :] = kv[:, src + NOPE_DIM:src + NOPE_DIM + V_DIM].T.astype(_BF16)
        vt_ref[hd, V_DIM:, :] = jnp.ones((VT_ROWS - V_DIM, rows), _BF16)


def _scores_t(q_ref, k_ref, hd, t):
    cols = slice(hd * QK_PAD, (hd + 1) * QK_PAD)
    rows = pl.ds(t * K_TILE, K_TILE) if isinstance(t, int) else pl.ds(
        pl.multiple_of(t * K_TILE, K_TILE), K_TILE)
    return jax.lax.dot_general(k_ref[rows, cols], q_ref[:, cols], (((1,), (1,)), ((), ())),
                               preferred_element_type=_F32)


def _store_head(o_ref, hd, acc):
    o_ref[:, hd * V_DIM:(hd + 1) * V_DIM] = (
        acc[:V_DIM] / acc[V_DIM:V_DIM + 1]).T.astype(o_ref.dtype)


def _attend_kernel(q_ref, k_ref, vt_ref, o_ref):
    n_tiles = k_ref.shape[0] // K_TILE
    tiles = [(hd, t) for hd in range(B_HEADS) for t in range(n_tiles)]

    jump = jnp.full((1, Q_TILE), -jnp.inf, _F32)
    ahead = [_scores_t(q_ref, k_ref, *tl) for tl in tiles[:K_AHEAD]]
    for i, (hd, t) in enumerate(tiles):
        st = ahead.pop(0)
        if i + K_AHEAD < len(tiles):
            ahead.append(_scores_t(q_ref, k_ref, *tiles[i + K_AHEAD]))
        tile_max = jnp.max(st, axis=0, keepdims=True)
        if t == 0:
            m = tile_max
            acc = jnp.dot(vt_ref[hd, t], jnp.exp2(st - m).astype(_BF16),
                          preferred_element_type=_F32)
        else:
            jump = jnp.maximum(jump, tile_max - m)
            acc = alpha * acc + jnp.dot(vt_ref[hd, t], jnp.exp2(st - m).astype(_BF16),
                                        preferred_element_type=_F32)
        m_new = jnp.maximum(m, tile_max)
        alpha = jnp.exp2(m - m_new)
        m = m_new
        if t == n_tiles - 1:
            _store_head(o_ref, hd, acc)

    @pl.when(jnp.max(jump) > LAG_LIMIT)
    def _():
        for hd in range(B_HEADS):
            def tile_step(t, carry, hd=hd):
                m, acc = carry
                st = _scores_t(q_ref, k_ref, hd, t)
                m_new = jnp.maximum(m, jnp.max(st, axis=0, keepdims=True))
                acc = jnp.exp2(m - m_new) * acc + jnp.dot(
                    vt_ref[hd, t], jnp.exp2(st - m_new).astype(_BF16),
                    preferred_element_type=_F32)
                return m_new, acc

            init = (jnp.full((1, Q_TILE), -jnp.inf, _F32), jnp.zeros((VT_ROWS, Q_TILE), _F32))
            _store_head(o_ref, hd, jax.lax.fori_loop(0, n_tiles, tile_step, init)[1])


def _ffn_out_kernel(x_ref, ya_ref, yb_ref, g_b_ref, w_out_ref, g_ffn_ref, w1_ref, w2_ref,
                    g_fin_ref, o_ref, h_ref):
    yb = _rms(yb_ref[...], g_b_ref[...]).astype(_BF16)
    x1 = (x_ref[...]
          + jnp.dot(ya_ref[...], w_out_ref[:A_WIDTH, :], preferred_element_type=_F32)
          + jnp.dot(yb, w_out_ref[A_WIDTH:, :], preferred_element_type=_F32))
    o_ref[...] = x1
    h_ref[...] = _rms(x1, g_ffn_ref[...]).astype(_BF16)

    def chunk(c, carry):
        f = jnp.maximum(jnp.dot(h_ref[...], w1_ref[c], preferred_element_type=_F32), 0.0)
        o_ref[...] += jnp.dot((f * f).astype(_BF16), w2_ref[c], preferred_element_type=_F32)
        return carry

    jax.lax.fori_loop(0, D_FF // FF_CHUNK, chunk, 0)
    o_ref[...] = _rms(o_ref[...], g_fin_ref[...])


def _const_spec(shape):
    nd = len(shape)
    return pl.BlockSpec(shape, lambda *_: (0,) * nd, pipeline_mode=pl.Buffered(1))


def _rope_tables(seq):
    inv = 1.0 / (ROPE_BASE ** (jnp.arange(0, ROPE_DIM, 2, dtype=_F32) / ROPE_DIM))
    ang = jnp.arange(seq, dtype=_F32)[:, None] * inv[None, :]
    cos, sin = jnp.cos(ang), jnp.sin(ang)
    zero = jnp.zeros_like(cos)
    pad = jnp.zeros((seq, LANES - ROPE_DIM), _F32)
    cos_t = jnp.concatenate([cos, cos, pad], axis=1)
    sin_lo = jnp.concatenate([-sin, zero, pad], axis=1)
    sin_hi = jnp.concatenate([zero, sin, pad], axis=1)
    return cos_t, sin_lo, sin_hi


def _attend(q, k, vt):
    bsz, _, k_tiles, _, _ = vt.shape
    seq = k_tiles * K_TILE
    q_tiles = seq // Q_TILE
    return pl.pallas_call(
        _attend_kernel,
        grid=(bsz, q_tiles),
        in_specs=[pl.BlockSpec((Q_TILE, B_HEADS * QK_PAD), lambda b, i: (b * q_tiles + i, 0)),
                  pl.BlockSpec((seq, B_HEADS * QK_PAD), lambda b, i: (b, 0)),
                  pl.BlockSpec((None, B_HEADS, k_tiles, VT_ROWS, K_TILE),
                               lambda b, i: (b, 0, 0, 0, 0))],
        out_specs=pl.BlockSpec((Q_TILE, B_WIDTH), lambda b, i: (b * q_tiles + i, 0)),
        out_shape=jax.ShapeDtypeStruct((bsz * seq, B_WIDTH), _F32),
        compiler_params=pltpu.CompilerParams(
            dimension_semantics=("arbitrary", "arbitrary"),
            vmem_limit_bytes=VMEM_LIMIT_BYTES),
        name="attend",
    )(q, k, vt)


def _trunk(x, p):
    bsz, seq, _ = x.shape
    n_tok = bsz * seq
    x2 = x.reshape(n_tok, D_MODEL)
    tiles_per_seq = seq // TOKEN_TILE
    cos_t, sin_lo, sin_hi = _rope_tables(seq)

    tok = lambda width: pl.BlockSpec((TOKEN_TILE, width), lambda i: (i, 0))
    rope = pl.BlockSpec((TOKEN_TILE, LANES), lambda i: (i % tiles_per_seq, 0))
    params = pltpu.CompilerParams(dimension_semantics=("arbitrary",),
                                  vmem_limit_bytes=VMEM_LIMIT_BYTES)

    ya, q, k, vt = pl.pallas_call(
        _mix_in_kernel,
        grid=(n_tok // TOKEN_TILE,),
        in_specs=[tok(D_MODEL), rope, rope, rope,
                  _const_spec((1, D_MODEL)), _const_spec((D_MODEL, IN_COLS_PAD)),
                  _const_spec((1, A_WIDTH)), _const_spec((A_HEADS, CHUNK, CHUNK)),
                  _const_spec((CHUNK, A_HEADS)), _const_spec((1, Q_RANK)),
                  _const_spec((Q_RANK, B_HEADS * QK_PAD)), _const_spec((1, KV_RANK)),
                  _const_spec((KV_RANK, B_HEADS * (NOPE_DIM + V_DIM))),
                  _const_spec((1, A_WIDTH))],
        out_specs=[tok(A_WIDTH), tok(B_HEADS * QK_PAD), tok(B_HEADS * QK_PAD),
                   pl.BlockSpec((None, B_HEADS, None, VT_ROWS, TOKEN_TILE),
                                lambda i: (i // tiles_per_seq, 0, i % tiles_per_seq, 0, 0))],
        out_shape=[jax.ShapeDtypeStruct((n_tok, A_WIDTH), _BF16),
                   jax.ShapeDtypeStruct((n_tok, B_HEADS * QK_PAD), _BF16),
                   jax.ShapeDtypeStruct((n_tok, B_HEADS * QK_PAD), _BF16),
                   jax.ShapeDtypeStruct((bsz, B_HEADS, tiles_per_seq, VT_ROWS, TOKEN_TILE),
                                        _BF16)],
        compiler_params=params,
        name="mix_in",
    )(x2, cos_t, sin_lo, sin_hi, p["g_mix"], p["w_in"], p["g_sgu"], p["w_sp"], p["b_sp"],
      p["g_q"], p["w_uq"], p["g_kv"], p["w_ukv"], p["g_a"])

    yb = _attend(q, k, vt)

    out = pl.pallas_call(
        _ffn_out_kernel,
        grid=(n_tok // TOKEN_TILE,),
        in_specs=[tok(D_MODEL), tok(A_WIDTH), tok(B_WIDTH), _const_spec((1, B_WIDTH)),
                  _const_spec((D_MODEL, D_MODEL)), _const_spec((1, D_MODEL)),
                  _const_spec((D_FF // FF_CHUNK, D_MODEL, FF_CHUNK)),
                  _const_spec((D_FF // FF_CHUNK, FF_CHUNK, D_MODEL)),
                  _const_spec((1, D_MODEL))],
        out_specs=tok(D_MODEL),
        out_shape=jax.ShapeDtypeStruct((n_tok, D_MODEL), _F32),
        scratch_shapes=[pltpu.VMEM((TOKEN_TILE, D_MODEL), _BF16)],
        compiler_params=params,
        name="ffn_out",
    )(x2, ya, yb, p["g_b"], p["w_out"], p["g_ffn"], p["w_ff1"], p["w_ff2"], p["g_fin"])
    return out.reshape(bsz, seq, D_MODEL)


def _prepare_params(norm_mix, w_in, sgu_norm, w_spatial, b_spatial, q_norm, w_uq, kv_norm,
                    w_ukv, out_norm_a, out_norm_b, w_out, norm_ffn, w_ff1, w_ff2, norm_final):
    w_in_p = jnp.pad(w_in[0], ((0, 0), (0, IN_COLS_PAD - IN_COLS))).astype(_BF16)
    uq = w_uq[0].reshape(Q_RANK, B_HEADS, QK_DIM)
    uq = jnp.pad(uq, ((0, 0), (0, 0), (0, QK_PAD - QK_DIM)))
    row = lambda g: g.reshape(1, -1).astype(_F32)
    n_chunks = D_FF // FF_CHUNK
    w1 = w_ff1[0].astype(_BF16).reshape(D_MODEL, n_chunks, FF_CHUNK).transpose(1, 0, 2)
    w2 = w_ff2[0].astype(_BF16).reshape(n_chunks, FF_CHUNK, D_MODEL)
    return dict(
        g_mix=row(norm_mix[0]), w_in=w_in_p, g_sgu=row(sgu_norm[0]),
        w_sp=w_spatial[0].astype(_BF16), b_sp=b_spatial[0].T.astype(_F32),
        g_q=row(q_norm[0]), w_uq=uq.reshape(Q_RANK, B_HEADS * QK_PAD).astype(_BF16),
        g_kv=row(kv_norm[0]), w_ukv=w_ukv[0].astype(_BF16), g_a=row(out_norm_a[0]),
        g_b=row(out_norm_b[0]), w_out=w_out[0].astype(_BF16), g_ffn=row(norm_ffn[0]),
        w_ff1=w1, w_ff2=w2, g_fin=row(norm_final))


def kernel(x_prompt, x_sample, norm_mix, w_in, sgu_norm, w_spatial, b_spatial, q_norm, w_uq,
           kv_norm, w_ukv, out_norm_a, out_norm_b, w_out, norm_ffn, w_ff1, w_ff2, norm_final):
    assert norm_mix.shape[0] == 1, "single-layer trunk"
    p = _prepare_params(norm_mix, w_in, sgu_norm, w_spatial, b_spatial, q_norm, w_uq, kv_norm,
                        w_ukv, out_norm_a, out_norm_b, w_out, norm_ffn, w_ff1, w_ff2,
                        norm_final)
    return (_trunk(x_prompt, p), _trunk(x_sample, p))
```

```python
import math

import jax
import jax.numpy as jnp
from jax.experimental import pallas as pl
from jax.experimental.pallas import tpu as pltpu

LANES = 128
D_MODEL = 1024
A_WIDTH = 512
A_HEADS = 4
A_HEAD_DIM = A_WIDTH // A_HEADS
CHUNK = 128
B_HEADS = 4
NOPE_DIM = 128
ROPE_DIM = 64
V_DIM = 128
QK_DIM = NOPE_DIM + ROPE_DIM
QK_PAD = 2 * LANES
VT_ROWS = V_DIM + 16
B_WIDTH = B_HEADS * V_DIM
Q_RANK = 384
KV_RANK = 256
IN_COLS = 2 * A_WIDTH + Q_RANK + KV_RANK + ROPE_DIM
IN_COLS_PAD = 2 * A_WIDTH + Q_RANK + KV_RANK + LANES
D_FF = 4 * D_MODEL
ROPE_BASE = 10000.0
EPS = 1e-6
ATTN_SCALE = QK_DIM ** -0.5
Q_SCALE = ATTN_SCALE * math.log2(math.e)
VMEM_LIMIT_BYTES = 56 * 1024 * 1024

TOKEN_TILE = 512
MIX_SUBTILES = 2
FFN_TILE = 1024
Q_TILE = 512
K_TILE = 512
K_AHEAD = 1
LAG_LIMIT = 64.0
assert K_TILE == TOKEN_TILE
FF_CHUNK = 1024

_BF16 = jnp.bfloat16
_F32 = jnp.float32


def _rms(x, gain):
    return x * jax.lax.rsqrt(jnp.mean(x * x, axis=-1, keepdims=True) + EPS) * gain


def _gelu_tanh(x):
    c = math.sqrt(2.0 / math.pi)
    return 0.5 * x * (1.0 + jnp.tanh(c * (x + 0.044715 * (x * x * x))))


def _rope_padded(x, cos_t, sin_lo, sin_hi):
    half = ROPE_DIM // 2
    return (x * cos_t + pltpu.roll(x, LANES - half, 1) * sin_lo
            + pltpu.roll(x, half, 1) * sin_hi)


def _mix_in_kernel(x_ref, cos_ref, slo_ref, shi_ref, g_mix_ref, w_in_ref, g_sgu_ref,
                   w_sp_ref, b_sp_ref, g_q_ref, w_uq_ref, g_kv_ref, w_ukv_ref, g_a_ref,
                   ya_ref, q_ref, k_ref, vt_ref):
    sub_rows = [slice(r * TOKEN_TILE, (r + 1) * TOKEN_TILE) for r in range(MIX_SUBTILES)]
    zs = [jnp.dot(_rms(x_ref[rs, :], g_mix_ref[...]).astype(_BF16), w_in_ref[...],
                  preferred_element_type=_F32) for rs in sub_rows]
    for r, rs in enumerate(sub_rows):
        _mix_in_rows(zs[r], cos_ref[rs, :], slo_ref[rs, :], shi_ref[rs, :], g_sgu_ref, w_sp_ref,
                     b_sp_ref, g_q_ref, w_uq_ref, g_kv_ref, w_ukv_ref, g_a_ref,
                     ya_ref.at[rs, :], q_ref.at[rs, :], k_ref.at[rs, :], vt_ref.at[:, r])


def _mix_in_rows(z, cos_t, sin_lo, sin_hi, g_sgu_ref, w_sp_ref, b_sp_ref, g_q_ref, w_uq_ref,
                 g_kv_ref, w_ukv_ref, g_a_ref, ya_ref, q_ref, k_ref, vt_ref):
    rows = z.shape[0]
    u = _gelu_tanh(z[:, :A_WIDTH])
    v = _gelu_tanh(z[:, A_WIDTH:2 * A_WIDTH])

    ya_cols = []
    for hd in range(A_HEADS):
        cols = slice(hd * A_HEAD_DIM, (hd + 1) * A_HEAD_DIM)
        vn = _rms(v[:, cols], g_sgu_ref[:, cols]).astype(_BF16)
        bias = jnp.broadcast_to(b_sp_ref[:, hd:hd + 1], (CHUNK, A_HEAD_DIM))
        parts = []
        for c in range(rows // CHUNK):
            s = jnp.dot(w_sp_ref[hd], vn[c * CHUNK:(c + 1) * CHUNK],
                        preferred_element_type=_F32) + bias
            parts.append(s)
        ya_cols.append(u[:, cols] * jnp.concatenate(parts, axis=0))
    ya = jnp.concatenate(ya_cols, axis=1)
    ya_ref[...] = _rms(ya, g_a_ref[...]).astype(_BF16)

    o = 2 * A_WIDTH
    cq = _rms(z[:, o:o + Q_RANK], g_q_ref[...]).astype(_BF16)
    q = jnp.dot(cq, w_uq_ref[...], preferred_element_type=_F32)
    for hd in range(B_HEADS):
        base = hd * QK_PAD
        q_ref[:, base:base + LANES] = (q[:, base:base + LANES] * Q_SCALE).astype(_BF16)
        pe = _rope_padded(q[:, base + LANES:base + QK_PAD], cos_t, sin_lo, sin_hi)
        q_ref[:, base + LANES:base + QK_PAD] = (pe * Q_SCALE).astype(_BF16)

    o += Q_RANK
    ckv = _rms(z[:, o:o + KV_RANK], g_kv_ref[...]).astype(_BF16)
    kv = jnp.dot(ckv, w_ukv_ref[...], preferred_element_type=_F32)
    o += KV_RANK
    k_pe = _rope_padded(z[:, o:o + LANES], cos_t, sin_lo, sin_hi).astype(_BF16)
    for hd in range(B_HEADS):
        src = hd * (NOPE_DIM + V_DIM)
        k_ref[:, hd * QK_PAD:hd * QK_PAD + LANES] = kv[:, src:src + NOPE_DIM].astype(_BF16)
        k_ref[:, hd * QK_PAD + LANES:(hd + 1) * QK_PAD] = k_pe
        vt_ref[hd, :V_DIM, :] = kv[:, src + NOPE_DIM:src + NOPE_DIM + V_DIM].T.astype(_BF16)
        vt_ref[hd, V_DIM:, :] = jnp.ones((VT_ROWS - V_DIM, rows), _BF16)


def _scores_t(q_ref, k_ref, hd, t):
    cols = slice(hd * QK_PAD, (hd + 1) * QK_PAD)
    rows = pl.ds(t * K_TILE, K_TILE) if isinstance(t, int) else pl.ds(
        pl.multiple_of(t * K_TILE, K_TILE), K_TILE)
    return jax.lax.dot_general(k_ref[rows, cols], q_ref[:, cols], (((1,), (1,)), ((), ())),
                               preferred_element_type=_F32)


def _store_head(o_ref, hd, acc):
    o_ref[:, hd * V_DIM:(hd + 1) * V_DIM] = (
        acc[:V_DIM] / acc[V_DIM:V_DIM + 1]).T.astype(o_ref.dtype)


def _attend_kernel(q_ref, k_ref, vt_ref, o_ref):
    n_tiles = k_ref.shape[0] // K_TILE
    tiles = [(hd, t) for hd in range(B_HEADS) for t in range(n_tiles)]

    jump = jnp.full((1, Q_TILE), -jnp.inf, _F32)
    ahead = [_scores_t(q_ref, k_ref, *tl) for tl in tiles[:K_AHEAD]]
    for i, (hd, t) in enumerate(tiles):
        st = ahead.pop(0)
        if i + K_AHEAD < len(tiles):
            ahead.append(_scores_t(q_ref, k_ref, *tiles[i + K_AHEAD]))
        tile_max = jnp.max(st, axis=0, keepdims=True)
        if t == 0:
            m = tile_max
            acc = jnp.dot(vt_ref[hd, t], jnp.exp2(st - m).astype(_BF16),
                          preferred_element_type=_F32)
        else:
            jump = jnp.maximum(jump, tile_max - m)
            acc = alpha * acc + jnp.dot(vt_ref[hd, t], jnp.exp2(st - m).astype(_BF16),
                                        preferred_element_type=_F32)
        m_new = jnp.maximum(m, tile_max)
        alpha = jnp.exp2(m - m_new)
        m = m_new
        if t == n_tiles - 1:
            _store_head(o_ref, hd, acc)

    @pl.when(jnp.max(jump) > LAG_LIMIT)
    def _():
        for hd in range(B_HEADS):
            def tile_step(t, carry, hd=hd):
                m, acc = carry
                st = _scores_t(q_ref, k_ref, hd, t)
                m_new = jnp.maximum(m, jnp.max(st, axis=0, keepdims=True))
                acc = jnp.exp2(m - m_new) * acc + jnp.dot(
                    vt_ref[hd, t], jnp.exp2(st - m_new).astype(_BF16),
                    preferred_element_type=_F32)
                return m_new, acc

            init = (jnp.full((1, Q_TILE), -jnp.inf, _F32), jnp.zeros((VT_ROWS, Q_TILE), _F32))
            _store_head(o_ref, hd, jax.lax.fori_loop(0, n_tiles, tile_step, init)[1])


def _ffn_out_kernel(x_ref, ya_ref, yb_ref, g_b_ref, w_out_ref, g_ffn_ref, w1_ref, w2_ref,
                    g_fin_ref, o_ref, h_ref):
    yb = _rms(yb_ref[...], g_b_ref[...]).astype(_BF16)
    x1 = (x_ref[...]
          + jnp.dot(ya_ref[...], w_out_ref[:A_WIDTH, :], preferred_element_type=_F32)
          + jnp.dot(yb, w_out_ref[A_WIDTH:, :], preferred_element_type=_F32))
    o_ref[...] = x1
    h_ref[...] = _rms(x1, g_ffn_ref[...]).astype(_BF16)

    def chunk(c, carry):
        f = jnp.maximum(jnp.dot(h_ref[...], w1_ref[c], preferred_element_type=_F32), 0.0)
        o_ref[...] += jnp.dot((f * f).astype(_BF16), w2_ref[c], preferred_element_type=_F32)
        return carry

    jax.lax.fori_loop(0, D_FF // FF_CHUNK, chunk, 0, unroll=True)
    o_ref[...] = _rms(o_ref[...], g_fin_ref[...])


def _const_spec(shape):
    nd = len(shape)
    return pl.BlockSpec(shape, lambda *_: (0,) * nd, pipeline_mode=pl.Buffered(1))


def _rope_tables(seq):
    inv = 1.0 / (ROPE_BASE ** (jnp.arange(0, ROPE_DIM, 2, dtype=_F32) / ROPE_DIM))
    ang = jnp.arange(seq, dtype=_F32)[:, None] * inv[None, :]
    cos, sin = jnp.cos(ang), jnp.sin(ang)
    zero = jnp.zeros_like(cos)
    pad = jnp.zeros((seq, LANES - ROPE_DIM), _F32)
    cos_t = jnp.concatenate([cos, cos, pad], axis=1)
    sin_lo = jnp.concatenate([-sin, zero, pad], axis=1)
    sin_hi = jnp.concatenate([zero, sin, pad], axis=1)
    return cos_t, sin_lo, sin_hi


def _attend(q, k, vt):
    bsz, _, k_tiles, _, _ = vt.shape
    seq = k_tiles * K_TILE
    q_tiles = seq // Q_TILE
    return pl.pallas_call(
        _attend_kernel,
        grid=(bsz, q_tiles),
        in_specs=[pl.BlockSpec((Q_TILE, B_HEADS * QK_PAD), lambda b, i: (b * q_tiles + i, 0)),
                  pl.BlockSpec((seq, B_HEADS * QK_PAD), lambda b, i: (b, 0)),
                  pl.BlockSpec((None, B_HEADS, k_tiles, VT_ROWS, K_TILE),
                               lambda b, i: (b, 0, 0, 0, 0))],
        out_specs=pl.BlockSpec((Q_TILE, B_WIDTH), lambda b, i: (b * q_tiles + i, 0)),
        out_shape=jax.ShapeDtypeStruct((bsz * seq, B_WIDTH), _F32),
        compiler_params=pltpu.CompilerParams(
            dimension_semantics=("arbitrary", "arbitrary"),
            vmem_limit_bytes=VMEM_LIMIT_BYTES),
        name="attend",
    )(q, k, vt)


def _trunk(x, p):
    bsz, seq, _ = x.shape
    n_tok = bsz * seq
    x2 = x.reshape(n_tok, D_MODEL)
    mix_rows = TOKEN_TILE * MIX_SUBTILES
    steps_per_seq = seq // mix_rows
    cos_t, sin_lo, sin_hi = _rope_tables(seq)

    tok = lambda width: pl.BlockSpec((mix_rows, width), lambda i: (i, 0))
    rope = pl.BlockSpec((mix_rows, LANES), lambda i: (i % steps_per_seq, 0))
    params = pltpu.CompilerParams(dimension_semantics=("arbitrary",),
                                  vmem_limit_bytes=VMEM_LIMIT_BYTES)

    ya, q, k, vt = pl.pallas_call(
        _mix_in_kernel,
        grid=(n_tok // mix_rows,),
        in_specs=[tok(D_MODEL), rope, rope, rope,
                  _const_spec((1, D_MODEL)), _const_spec((D_MODEL, IN_COLS_PAD)),
                  _const_spec((1, A_WIDTH)), _const_spec((A_HEADS, CHUNK, CHUNK)),
                  _const_spec((CHUNK, A_HEADS)), _const_spec((1, Q_RANK)),
                  _const_spec((Q_RANK, B_HEADS * QK_PAD)), _const_spec((1, KV_RANK)),
                  _const_spec((KV_RANK, B_HEADS * (NOPE_DIM + V_DIM))),
                  _const_spec((1, A_WIDTH))],
        out_specs=[tok(A_WIDTH), tok(B_HEADS * QK_PAD), tok(B_HEADS * QK_PAD),
                   pl.BlockSpec((None, B_HEADS, MIX_SUBTILES, VT_ROWS, TOKEN_TILE),
                                lambda i: (i // steps_per_seq, 0, i % steps_per_seq, 0, 0))],
        out_shape=[jax.ShapeDtypeStruct((n_tok, A_WIDTH), _BF16),
                   jax.ShapeDtypeStruct((n_tok, B_HEADS * QK_PAD), _BF16),
                   jax.ShapeDtypeStruct((n_tok, B_HEADS * QK_PAD), _BF16),
                   jax.ShapeDtypeStruct((bsz, B_HEADS, seq // TOKEN_TILE, VT_ROWS, TOKEN_TILE),
                                        _BF16)],
        compiler_params=params,
        name="mix_in",
    )(x2, cos_t, sin_lo, sin_hi, p["g_mix"], p["w_in"], p["g_sgu"], p["w_sp"], p["b_sp"],
      p["g_q"], p["w_uq"], p["g_kv"], p["w_ukv"], p["g_a"])

    yb = _attend(q, k, vt)

    ftok = lambda width: pl.BlockSpec((FFN_TILE, width), lambda i: (i, 0))
    out = pl.pallas_call(
        _ffn_out_kernel,
        grid=(n_tok // FFN_TILE,),
        in_specs=[ftok(D_MODEL), ftok(A_WIDTH), ftok(B_WIDTH), _const_spec((1, B_WIDTH)),
                  _const_spec((D_MODEL, D_MODEL)), _const_spec((1, D_MODEL)),
                  _const_spec((D_FF // FF_CHUNK, D_MODEL, FF_CHUNK)),
                  _const_spec((D_FF // FF_CHUNK, FF_CHUNK, D_MODEL)),
                  _const_spec((1, D_MODEL))],
        out_specs=ftok(D_MODEL),
        out_shape=jax.ShapeDtypeStruct((n_tok, D_MODEL), _F32),
        scratch_shapes=[pltpu.VMEM((FFN_TILE, D_MODEL), _BF16)],
        compiler_params=params,
        name="ffn_out",
    )(x2, ya, yb, p["g_b"], p["w_out"], p["g_ffn"], p["w_ff1"], p["w_ff2"], p["g_fin"])
    return out.reshape(bsz, seq, D_MODEL)


def _prepare_params(norm_mix, w_in, sgu_norm, w_spatial, b_spatial, q_norm, w_uq, kv_norm,
                    w_ukv, out_norm_a, out_norm_b, w_out, norm_ffn, w_ff1, w_ff2, norm_final):
    w_in_p = jnp.pad(w_in[0], ((0, 0), (0, IN_COLS_PAD - IN_COLS))).astype(_BF16)
    uq = w_uq[0].reshape(Q_RANK, B_HEADS, QK_DIM)
    uq = jnp.pad(uq, ((0, 0), (0, 0), (0, QK_PAD - QK_DIM)))
    row = lambda g: g.reshape(1, -1).astype(_F32)
    n_chunks = D_FF // FF_CHUNK
    w1 = w_ff1[0].astype(_BF16).reshape(D_MODEL, n_chunks, FF_CHUNK).transpose(1, 0, 2)
    w2 = w_ff2[0].astype(_BF16).reshape(n_chunks, FF_CHUNK, D_MODEL)
    return dict(
        g_mix=row(norm_mix[0]), w_in=w_in_p, g_sgu=row(sgu_norm[0]),
        w_sp=w_spatial[0].astype(_BF16), b_sp=b_spatial[0].T.astype(_F32),
        g_q=row(q_norm[0]), w_uq=uq.reshape(Q_RANK, B_HEADS * QK_PAD).astype(_BF16),
        g_kv=row(kv_norm[0]), w_ukv=w_ukv[0].astype(_BF16), g_a=row(out_norm_a[0]),
        g_b=row(out_norm_b[0]), w_out=w_out[0].astype(_BF16), g_ffn=row(norm_ffn[0]),
        w_ff1=w1, w_ff2=w2, g_fin=row(norm_final))


def kernel(x_prompt, x_sample, norm_mix, w_in, sgu_norm, w_spatial, b_spatial, q_norm, w_uq,
           kv_norm, w_ukv, out_norm_a, out_norm_b, w_out, norm_ffn, w_ff1, w_ff2, norm_final):
    assert norm_mix.shape[0] == 1, "single-layer trunk"
    p = _prepare_params(norm_mix, w_in, sgu_norm, w_spatial, b_spatial, q_norm, w_uq, kv_norm,
                        w_ukv, out_norm_a, out_norm_b, w_out, norm_ffn, w_ff1, w_ff2,
                        norm_final)
    return (_trunk(x_prompt, p), _trunk(x_sample, p))
```

```python
import math

import jax
import jax.numpy as jnp
from jax.experimental import pallas as pl
from jax.experimental.pallas import tpu as pltpu

LANES = 128
D_MODEL = 1024
A_WIDTH = 512
A_HEADS = 4
A_HEAD_DIM = A_WIDTH // A_HEADS
CHUNK = 128
B_HEADS = 4
NOPE_DIM = 128
ROPE_DIM = 64
V_DIM = 128
QK_DIM = NOPE_DIM + ROPE_DIM
QK_PAD = 2 * LANES
VT_ROWS = V_DIM + 16
B_WIDTH = B_HEADS * V_DIM
Q_RANK = 384
KV_RANK = 256
IN_COLS = 2 * A_WIDTH + Q_RANK + KV_RANK + ROPE_DIM
IN_COLS_PAD = 2 * A_WIDTH + Q_RANK + KV_RANK + LANES
D_FF = 4 * D_MODEL
ROPE_BASE = 10000.0
EPS = 1e-6
ATTN_SCALE = QK_DIM ** -0.5
Q_SCALE = ATTN_SCALE * math.log2(math.e)
VMEM_LIMIT_BYTES = 56 * 1024 * 1024

MIX_ROWS = 1024
MIX_SUB = 256
FFN_TILE = 1024
Q_TILE = 512
K_TILE = 512
K_SUB = 256
K_AHEAD = 2
LAG_LIMIT = 64.0
assert MIX_ROWS % K_TILE == 0 and K_TILE % MIX_SUB == 0
FF_CHUNK = 1024

_BF16 = jnp.bfloat16
_F32 = jnp.float32


def _rms(x, gain):
    return x * jax.lax.rsqrt(jnp.mean(x * x, axis=-1, keepdims=True) + EPS) * gain


def _gelu_tanh(x):
    c = math.sqrt(2.0 / math.pi)
    return 0.5 * x * (1.0 + jnp.tanh(c * (x + 0.044715 * (x * x * x))))


def _rope_padded(x, cos_t, sin_lo, sin_hi):
    half = ROPE_DIM // 2
    return (x * cos_t + pltpu.roll(x, LANES - half, 1) * sin_lo
            + pltpu.roll(x, half, 1) * sin_hi)


def _mix_in_kernel(x_ref, cos_ref, slo_ref, shi_ref, g_mix_ref, w_in_ref, g_sgu_ref,
                   w_sp_ref, b_sp_ref, g_q_ref, w_uq_ref, g_kv_ref, w_ukv_ref, g_a_ref,
                   ya_ref, q_ref, k_ref, vt_ref):
    starts = range(0, MIX_ROWS, MIX_SUB)
    zs = [jnp.dot(_rms(x_ref[r0:r0 + MIX_SUB, :], g_mix_ref[...]).astype(_BF16), w_in_ref[...],
                  preferred_element_type=_F32) for r0 in starts]
    for z, r0 in zip(zs, starts):
        rs = slice(r0, r0 + MIX_SUB)
        kt, off = divmod(r0, K_TILE)
        _mix_in_rows(z, cos_ref[rs, :], slo_ref[rs, :], shi_ref[rs, :], g_sgu_ref, w_sp_ref,
                     b_sp_ref, g_q_ref, w_uq_ref, g_kv_ref, w_ukv_ref, g_a_ref,
                     ya_ref.at[rs, :], q_ref.at[rs, :], k_ref.at[rs, :],
                     vt_ref.at[:, kt, :, off:off + MIX_SUB])


def _mix_in_rows(z, cos_t, sin_lo, sin_hi, g_sgu_ref, w_sp_ref, b_sp_ref, g_q_ref, w_uq_ref,
                 g_kv_ref, w_ukv_ref, g_a_ref, ya_ref, q_ref, k_ref, vt_ref):
    rows = z.shape[0]
    u = _gelu_tanh(z[:, :A_WIDTH])
    v = _gelu_tanh(z[:, A_WIDTH:2 * A_WIDTH])

    ya_cols = []
    for hd in range(A_HEADS):
        cols = slice(hd * A_HEAD_DIM, (hd + 1) * A_HEAD_DIM)
        vn = _rms(v[:, cols], g_sgu_ref[:, cols]).astype(_BF16)
        bias = jnp.broadcast_to(b_sp_ref[:, hd:hd + 1], (CHUNK, A_HEAD_DIM))
        parts = []
        for c in range(rows // CHUNK):
            s = jnp.dot(w_sp_ref[hd], vn[c * CHUNK:(c + 1) * CHUNK],
                        preferred_element_type=_F32) + bias
            parts.append(s)
        ya_cols.append(u[:, cols] * jnp.concatenate(parts, axis=0))
    ya = jnp.concatenate(ya_cols, axis=1)
    ya_ref[...] = _rms(ya, g_a_ref[...]).astype(_BF16)

    o = 2 * A_WIDTH
    cq = _rms(z[:, o:o + Q_RANK], g_q_ref[...]).astype(_BF16)
    q = jnp.dot(cq, w_uq_ref[...], preferred_element_type=_F32)
    for hd in range(B_HEADS):
        base = hd * QK_PAD
        q_ref[:, base:base + LANES] = (q[:, base:base + LANES] * Q_SCALE).astype(_BF16)
        pe = _rope_padded(q[:, base + LANES:base + QK_PAD], cos_t, sin_lo, sin_hi)
        q_ref[:, base + LANES:base + QK_PAD] = (pe * Q_SCALE).astype(_BF16)

    o += Q_RANK
    ckv = _rms(z[:, o:o + KV_RANK], g_kv_ref[...]).astype(_BF16)
    kv = jnp.dot(ckv, w_ukv_ref[...], preferred_element_type=_F32)
    o += KV_RANK
    k_pe = _rope_padded(z[:, o:o + LANES], cos_t, sin_lo, sin_hi).astype(_BF16)
    for hd in range(B_HEADS):
        src = hd * (NOPE_DIM + V_DIM)
        k_ref[:, hd * QK_PAD:hd * QK_PAD + LANES] = kv[:, src:src + NOPE_DIM].astype(_BF16)
        k_ref[:, hd * QK_PAD + LANES:(hd + 1) * QK_PAD] = k_pe
        vt_ref[hd, :V_DIM, :] = kv[:, src + NOPE_DIM:src + NOPE_DIM + V_DIM].T.astype(_BF16)
        vt_ref[hd, V_DIM:, :] = jnp.ones((VT_ROWS - V_DIM, rows), _BF16)


def _scores_t(q_ref, k_ref, hd, row0, n_rows):
    cols = slice(hd * QK_PAD, (hd + 1) * QK_PAD)
    if not isinstance(row0, int):
        row0 = pl.multiple_of(row0, n_rows)
    return jax.lax.dot_general(k_ref[pl.ds(row0, n_rows), cols], q_ref[:, cols],
                               (((1,), (1,)), ((), ())), preferred_element_type=_F32)


def _store_head(o_ref, hd, acc):
    o_ref[:, hd * V_DIM:(hd + 1) * V_DIM] = (
        acc[:V_DIM] / acc[V_DIM:V_DIM + 1]).T.astype(o_ref.dtype)


def _attend_kernel(q_ref, k_ref, vt_ref, o_ref):
    n_tiles = k_ref.shape[0] // K_TILE
    n_sub = K_TILE // K_SUB
    units = [(hd, t, u) for hd in range(B_HEADS) for t in range(n_tiles) for u in range(n_sub)]

    def scores_u(hd, t, u):
        return _scores_t(q_ref, k_ref, hd, t * K_TILE + u * K_SUB, K_SUB)

    jump = jnp.full((1, Q_TILE), -jnp.inf, _F32)
    ahead = [scores_u(*un) for un in units[:K_AHEAD]]
    for i, (hd, t, u) in enumerate(units):
        st = ahead.pop(0)
        if i + K_AHEAD < len(units):
            ahead.append(scores_u(*units[i + K_AHEAD]))
        sub_max = jnp.max(st, axis=0, keepdims=True)
        if t == 0 and u == 0:
            m = sub_max
        else:
            jump = jnp.maximum(jump, sub_max - m)
        pv = jnp.dot(vt_ref[hd, t, :, u * K_SUB:(u + 1) * K_SUB],
                     jnp.exp2(st - m).astype(_BF16), preferred_element_type=_F32)
        tile_acc = pv if u == 0 else tile_acc + pv
        tile_max = sub_max if u == 0 else jnp.maximum(tile_max, sub_max)
        if u == n_sub - 1:
            acc = tile_acc if t == 0 else alpha * acc + tile_acc
            m_new = jnp.maximum(m, tile_max)
            alpha = jnp.exp2(m - m_new)
            m = m_new
            if t == n_tiles - 1:
                _store_head(o_ref, hd, acc)

    @pl.when(jnp.max(jump) > LAG_LIMIT)
    def _():
        for hd in range(B_HEADS):
            def tile_step(t, carry, hd=hd):
                m, acc = carry
                st = _scores_t(q_ref, k_ref, hd, t * K_TILE, K_TILE)
                m_new = jnp.maximum(m, jnp.max(st, axis=0, keepdims=True))
                acc = jnp.exp2(m - m_new) * acc + jnp.dot(
                    vt_ref[hd, t], jnp.exp2(st - m_new).astype(_BF16),
                    preferred_element_type=_F32)
                return m_new, acc

            init = (jnp.full((1, Q_TILE), -jnp.inf, _F32), jnp.zeros((VT_ROWS, Q_TILE), _F32))
            _store_head(o_ref, hd, jax.lax.fori_loop(0, n_tiles, tile_step, init)[1])


def _ffn_out_kernel(x_ref, ya_ref, yb_ref, g_b_ref, w_out_ref, g_ffn_ref, w1_ref, w2_ref,
                    g_fin_ref, o_ref, h_ref):
    yb = _rms(yb_ref[...], g_b_ref[...]).astype(_BF16)
    x1 = (x_ref[...]
          + jnp.dot(ya_ref[...], w_out_ref[:A_WIDTH, :], preferred_element_type=_F32)
          + jnp.dot(yb, w_out_ref[A_WIDTH:, :], preferred_element_type=_F32))
    o_ref[...] = x1
    h_ref[...] = _rms(x1, g_ffn_ref[...]).astype(_BF16)
    for c in range(0, D_FF, FF_CHUNK):
        f = jnp.maximum(jnp.dot(h_ref[...], w1_ref[:, c:c + FF_CHUNK],
                                preferred_element_type=_F32), 0.0)
        o_ref[...] += jnp.dot((f * f).astype(_BF16), w2_ref[c:c + FF_CHUNK, :],
                              preferred_element_type=_F32)
    o_ref[...] = _rms(o_ref[...], g_fin_ref[...])


def _const_spec(shape):
    nd = len(shape)
    return pl.BlockSpec(shape, lambda *_: (0,) * nd, pipeline_mode=pl.Buffered(1))


def _rope_tables(seq):
    inv = 1.0 / (ROPE_BASE ** (jnp.arange(0, ROPE_DIM, 2, dtype=_F32) / ROPE_DIM))
    ang = jnp.arange(seq, dtype=_F32)[:, None] * inv[None, :]
    cos, sin = jnp.cos(ang), jnp.sin(ang)
    zero = jnp.zeros_like(cos)
    pad = jnp.zeros((seq, LANES - ROPE_DIM), _F32)
    cos_t = jnp.concatenate([cos, cos, pad], axis=1)
    sin_lo = jnp.concatenate([-sin, zero, pad], axis=1)
    sin_hi = jnp.concatenate([zero, sin, pad], axis=1)
    return cos_t, sin_lo, sin_hi


def _attend(q, k, vt):
    bsz, _, k_tiles, _, _ = vt.shape
    seq = k_tiles * K_TILE
    q_tiles = seq // Q_TILE
    return pl.pallas_call(
        _attend_kernel,
        grid=(bsz, q_tiles),
        in_specs=[pl.BlockSpec((Q_TILE, B_HEADS * QK_PAD), lambda b, i: (b * q_tiles + i, 0)),
                  pl.BlockSpec((seq, B_HEADS * QK_PAD), lambda b, i: (b, 0)),
                  pl.BlockSpec((None, B_HEADS, k_tiles, VT_ROWS, K_TILE),
                               lambda b, i: (b, 0, 0, 0, 0))],
        out_specs=pl.BlockSpec((Q_TILE, B_WIDTH), lambda b, i: (b * q_tiles + i, 0)),
        out_shape=jax.ShapeDtypeStruct((bsz * seq, B_WIDTH), _F32),
        compiler_params=pltpu.CompilerParams(
            dimension_semantics=("arbitrary", "arbitrary"),
            vmem_limit_bytes=VMEM_LIMIT_BYTES),
        name="attend",
    )(q, k, vt)


def _trunk(x, p):
    bsz, seq, _ = x.shape
    n_tok = bsz * seq
    x2 = x.reshape(n_tok, D_MODEL)
    steps_per_seq = seq // MIX_ROWS
    cos_t, sin_lo, sin_hi = _rope_tables(seq)

    tok = lambda width: pl.BlockSpec((MIX_ROWS, width), lambda i: (i, 0))
    rope = pl.BlockSpec((MIX_ROWS, LANES), lambda i: (i % steps_per_seq, 0))
    params = pltpu.CompilerParams(dimension_semantics=("arbitrary",),
                                  vmem_limit_bytes=VMEM_LIMIT_BYTES)

    ya, q, k, vt = pl.pallas_call(
        _mix_in_kernel,
        grid=(n_tok // MIX_ROWS,),
        in_specs=[tok(D_MODEL), rope, rope, rope,
                  _const_spec((1, D_MODEL)), _const_spec((D_MODEL, IN_COLS_PAD)),
                  _const_spec((1, A_WIDTH)), _const_spec((A_HEADS, CHUNK, CHUNK)),
                  _const_spec((CHUNK, A_HEADS)), _const_spec((1, Q_RANK)),
                  _const_spec((Q_RANK, B_HEADS * QK_PAD)), _const_spec((1, KV_RANK)),
                  _const_spec((KV_RANK, B_HEADS * (NOPE_DIM + V_DIM))),
                  _const_spec((1, A_WIDTH))],
        out_specs=[tok(A_WIDTH), tok(B_HEADS * QK_PAD), tok(B_HEADS * QK_PAD),
                   pl.BlockSpec((None, B_HEADS, MIX_ROWS // K_TILE, VT_ROWS, K_TILE),
                                lambda i: (i // steps_per_seq, 0, i % steps_per_seq, 0, 0))],
        out_shape=[jax.ShapeDtypeStruct((n_tok, A_WIDTH), _BF16),
                   jax.ShapeDtypeStruct((n_tok, B_HEADS * QK_PAD), _BF16),
                   jax.ShapeDtypeStruct((n_tok, B_HEADS * QK_PAD), _BF16),
                   jax.ShapeDtypeStruct((bsz, B_HEADS, seq // K_TILE, VT_ROWS, K_TILE),
                                        _BF16)],
        compiler_params=params,
        name="mix_in",
    )(x2, cos_t, sin_lo, sin_hi, p["g_mix"], p["w_in"], p["g_sgu"], p["w_sp"], p["b_sp"],
      p["g_q"], p["w_uq"], p["g_kv"], p["w_ukv"], p["g_a"])

    yb = _attend(q, k, vt)

    ftok = lambda width: pl.BlockSpec((FFN_TILE, width), lambda i: (i, 0))
    out = pl.pallas_call(
        _ffn_out_kernel,
        grid=(n_tok // FFN_TILE,),
        in_specs=[ftok(D_MODEL), ftok(A_WIDTH), ftok(B_WIDTH), _const_spec((1, B_WIDTH)),
                  _const_spec((D_MODEL, D_MODEL)), _const_spec((1, D_MODEL)),
                  _const_spec((D_MODEL, D_FF)), _const_spec((D_FF, D_MODEL)),
                  _const_spec((1, D_MODEL))],
        out_specs=ftok(D_MODEL),
        out_shape=jax.ShapeDtypeStruct((n_tok, D_MODEL), _F32),
        scratch_shapes=[pltpu.VMEM((FFN_TILE, D_MODEL), _BF16)],
        compiler_params=params,
        name="ffn_out",
    )(x2, ya, yb, p["g_b"], p["w_out"], p["g_ffn"], p["w_ff1"], p["w_ff2"], p["g_fin"])
    return out.reshape(bsz, seq, D_MODEL)


def _prepare_params(norm_mix, w_in, sgu_norm, w_spatial, b_spatial, q_norm, w_uq, kv_norm,
                    w_ukv, out_norm_a, out_norm_b, w_out, norm_ffn, w_ff1, w_ff2, norm_final):
    w_in_p = jnp.pad(w_in[0], ((0, 0), (0, IN_COLS_PAD - IN_COLS))).astype(_BF16)
    uq = w_uq[0].reshape(Q_RANK, B_HEADS, QK_DIM)
    uq = jnp.pad(uq, ((0, 0), (0, 0), (0, QK_PAD - QK_DIM)))
    row = lambda g: g.reshape(1, -1).astype(_F32)
    return dict(
        g_mix=row(norm_mix[0]), w_in=w_in_p, g_sgu=row(sgu_norm[0]),
        w_sp=w_spatial[0].astype(_BF16), b_sp=b_spatial[0].T.astype(_F32),
        g_q=row(q_norm[0]), w_uq=uq.reshape(Q_RANK, B_HEADS * QK_PAD).astype(_BF16),
        g_kv=row(kv_norm[0]), w_ukv=w_ukv[0].astype(_BF16), g_a=row(out_norm_a[0]),
        g_b=row(out_norm_b[0]), w_out=w_out[0].astype(_BF16), g_ffn=row(norm_ffn[0]),
        w_ff1=w_ff1[0].astype(_BF16), w_ff2=w_ff2[0].astype(_BF16), g_fin=row(norm_final))


def kernel(x_prompt, x_sample, norm_mix, w_in, sgu_norm, w_spatial, b_spatial, q_norm, w_uq,
           kv_norm, w_ukv, out_norm_a, out_norm_b, w_out, norm_ffn, w_ff1, w_ff2, norm_final):
    assert norm_mix.shape[0] == 1, "single-layer trunk"
    p = _prepare_params(norm_mix, w_in, sgu_norm, w_spatial, b_spatial, q_norm, w_uq, kv_norm,
                        w_ukv, out_norm_a, out_norm_b, w_out, norm_ffn, w_ff1, w_ff2,
                        norm_final)
    return (_trunk(x_prompt, p), _trunk(x_sample, p))
```

```python
import math

import jax
import jax.numpy as jnp
from jax.experimental import pallas as pl
from jax.experimental.pallas import tpu as pltpu

LANES = 128
D_MODEL = 1024
A_WIDTH = 512
A_HEADS = 4
A_HEAD_DIM = A_WIDTH // A_HEADS
CHUNK = 128
B_HEADS = 4
NOPE_DIM = 128
ROPE_DIM = 64
V_DIM = 128
QK_DIM = NOPE_DIM + ROPE_DIM
QK_PAD = 2 * LANES
VT_ROWS = V_DIM + 16
B_WIDTH = B_HEADS * V_DIM
Q_RANK = 384
KV_RANK = 256
IN_COLS = 2 * A_WIDTH + Q_RANK + KV_RANK + ROPE_DIM
IN_COLS_PAD = 2 * A_WIDTH + Q_RANK + KV_RANK + LANES
D_FF = 4 * D_MODEL
ROPE_BASE = 10000.0
EPS = 1e-6
ATTN_SCALE = QK_DIM ** -0.5
Q_SCALE = ATTN_SCALE * math.log2(math.e)
VMEM_LIMIT_BYTES = 56 * 1024 * 1024

MIX_ROWS = 1024
MIX_SUB = 256
FFN_TILE = 1024
Q_TILE = 512
K_TILE = 512
K_SUB = 256
K_AHEAD = 2
LAG_LIMIT = 64.0
assert MIX_ROWS % K_TILE == 0 and K_TILE % MIX_SUB == 0
FF_CHUNK = 1024

_BF16 = jnp.bfloat16
_F32 = jnp.float32


def _rms(x, gain):
    return x * jax.lax.rsqrt(jnp.mean(x * x, axis=-1, keepdims=True) + EPS) * gain


def _gelu_tanh(x):
    c = math.sqrt(2.0 / math.pi)
    return 0.5 * x * (1.0 + jnp.tanh(c * (x + 0.044715 * (x * x * x))))


def _rope_padded(x, cos_t, sin_lo, sin_hi):
    half = ROPE_DIM // 2
    return (x * cos_t + pltpu.roll(x, LANES - half, 1) * sin_lo
            + pltpu.roll(x, half, 1) * sin_hi)


def _mix_in_kernel(x_ref, cos_ref, slo_ref, shi_ref, g_mix_ref, w_in_ref, g_sgu_ref,
                   w_sp_ref, b_sp_ref, g_q_ref, w_uq_ref, g_kv_ref, w_ukv_ref, g_a_ref,
                   ya_ref, q_ref, k_ref, vt_ref):
    starts = range(0, MIX_ROWS, MIX_SUB)
    zs = [jnp.dot(_rms(x_ref[r0:r0 + MIX_SUB, :], g_mix_ref[...]).astype(_BF16), w_in_ref[...],
                  preferred_element_type=_F32) for r0 in starts]
    for z, r0 in zip(zs, starts):
        rs = slice(r0, r0 + MIX_SUB)
        kt, off = divmod(r0, K_TILE)
        _mix_in_rows(z, cos_ref[rs, :], slo_ref[rs, :], shi_ref[rs, :], g_sgu_ref, w_sp_ref,
                     b_sp_ref, g_q_ref, w_uq_ref, g_kv_ref, w_ukv_ref, g_a_ref,
                     ya_ref.at[rs, :], q_ref.at[rs, :], k_ref.at[rs, :],
                     vt_ref.at[:, kt, :, off:off + MIX_SUB])


def _mix_in_rows(z, cos_t, sin_lo, sin_hi, g_sgu_ref, w_sp_ref, b_sp_ref, g_q_ref, w_uq_ref,
                 g_kv_ref, w_ukv_ref, g_a_ref, ya_ref, q_ref, k_ref, vt_ref):
    rows = z.shape[0]
    u = _gelu_tanh(z[:, :A_WIDTH])
    v = _gelu_tanh(z[:, A_WIDTH:2 * A_WIDTH])

    ya_cols = []
    for hd in range(A_HEADS):
        cols = slice(hd * A_HEAD_DIM, (hd + 1) * A_HEAD_DIM)
        vn = _rms(v[:, cols], g_sgu_ref[:, cols]).astype(_BF16)
        bias = jnp.broadcast_to(b_sp_ref[:, hd:hd + 1], (CHUNK, A_HEAD_DIM))
        parts = []
        for c in range(rows // CHUNK):
            s = jnp.dot(w_sp_ref[hd], vn[c * CHUNK:(c + 1) * CHUNK],
                        preferred_element_type=_F32) + bias
            parts.append(s)
        ya_cols.append(u[:, cols] * jnp.concatenate(parts, axis=0))
    ya = jnp.concatenate(ya_cols, axis=1)
    ya_ref[...] = _rms(ya, g_a_ref[...]).astype(_BF16)

    o = 2 * A_WIDTH
    cq = _rms(z[:, o:o + Q_RANK], g_q_ref[...]).astype(_BF16)
    q = jnp.dot(cq, w_uq_ref[...], preferred_element_type=_F32)
    for hd in range(B_HEADS):
        base = hd * QK_PAD
        q_ref[:, base:base + LANES] = (q[:, base:base + LANES] * Q_SCALE).astype(_BF16)
        pe = _rope_padded(q[:, base + LANES:base + QK_PAD], cos_t, sin_lo, sin_hi)
        q_ref[:, base + LANES:base + QK_PAD] = (pe * Q_SCALE).astype(_BF16)

    o += Q_RANK
    ckv = _rms(z[:, o:o + KV_RANK], g_kv_ref[...]).astype(_BF16)
    kv = jnp.dot(ckv, w_ukv_ref[...], preferred_element_type=_F32)
    o += KV_RANK
    k_pe = _rope_padded(z[:, o:o + LANES], cos_t, sin_lo, sin_hi).astype(_BF16)
    for hd in range(B_HEADS):
        src = hd * (NOPE_DIM + V_DIM)
        k_ref[:, hd * QK_PAD:hd * QK_PAD + LANES] = kv[:, src:src + NOPE_DIM].astype(_BF16)
        k_ref[:, hd * QK_PAD + LANES:(hd + 1) * QK_PAD] = k_pe
        vt_ref[hd, :V_DIM, :] = kv[:, src + NOPE_DIM:src + NOPE_DIM + V_DIM].T.astype(_BF16)
        vt_ref[hd, V_DIM:, :] = jnp.ones((VT_ROWS - V_DIM, rows), _BF16)


def _scores_t(q_ref, k_ref, hd, row0, n_rows):
    cols = slice(hd * QK_PAD, (hd + 1) * QK_PAD)
    if not isinstance(row0, int):
        row0 = pl.multiple_of(row0, n_rows)
    return jax.lax.dot_general(k_ref[pl.ds(row0, n_rows), cols], q_ref[:, cols],
                               (((1,), (1,)), ((), ())), preferred_element_type=_F32)


def _store_head(o_ref, hd, acc):
    o_ref[:, hd * V_DIM:(hd + 1) * V_DIM] = (
        acc[:V_DIM] / acc[V_DIM:V_DIM + 1]).T.astype(o_ref.dtype)


def _attend_kernel(q_ref, k_ref, vt_ref, o_ref, qt_ref):
    n_tiles = k_ref.shape[0] // K_TILE
    n_sub = K_TILE // K_SUB
    units = [(hd, t, u) for hd in range(B_HEADS) for t in range(n_tiles) for u in range(n_sub)]

    for hd in range(B_HEADS):
        qt_ref[hd] = q_ref[:, hd * QK_PAD:(hd + 1) * QK_PAD].T

    def scores_u(hd, t, u):
        r0 = t * K_TILE + u * K_SUB
        return jnp.dot(k_ref[r0:r0 + K_SUB, hd * QK_PAD:(hd + 1) * QK_PAD], qt_ref[hd],
                       preferred_element_type=_F32)

    peak = jnp.zeros((1, Q_TILE), _F32)
    ahead = [scores_u(*un) for un in units[:K_AHEAD]]
    for i, (hd, t, u) in enumerate(units):
        st = ahead.pop(0)
        if i + K_AHEAD < len(units):
            ahead.append(scores_u(*units[i + K_AHEAD]))
        first = t == 0 and u == 0
        if first:
            shift = jnp.max(st, axis=0, keepdims=True)
        p = jnp.exp2(st - shift).astype(_BF16)
        if not first:
            peak = jnp.maximum(peak, jnp.max(p, axis=0, keepdims=True).astype(_F32))
        pv = jnp.dot(vt_ref[hd, t, :, u * K_SUB:(u + 1) * K_SUB], p,
                     preferred_element_type=_F32)
        acc = pv if first else acc + pv
        if t == n_tiles - 1 and u == n_sub - 1:
            _store_head(o_ref, hd, acc)

    @pl.when(jnp.max(peak) > 2.0 ** LAG_LIMIT)
    def _():
        for hd in range(B_HEADS):
            def tile_step(t, carry, hd=hd):
                m, acc = carry
                st = _scores_t(q_ref, k_ref, hd, t * K_TILE, K_TILE)
                m_new = jnp.maximum(m, jnp.max(st, axis=0, keepdims=True))
                acc = jnp.exp2(m - m_new) * acc + jnp.dot(
                    vt_ref[hd, t], jnp.exp2(st - m_new).astype(_BF16),
                    preferred_element_type=_F32)
                return m_new, acc

            init = (jnp.full((1, Q_TILE), -jnp.inf, _F32), jnp.zeros((VT_ROWS, Q_TILE), _F32))
            _store_head(o_ref, hd, jax.lax.fori_loop(0, n_tiles, tile_step, init)[1])


def _ffn_out_kernel(x_ref, ya_ref, yb_ref, g_b_ref, w_out_ref, g_ffn_ref, w1_ref, w2_ref,
                    g_fin_ref, o_ref, h_ref):
    yb = _rms(yb_ref[...], g_b_ref[...]).astype(_BF16)
    x1 = (x_ref[...]
          + jnp.dot(ya_ref[...], w_out_ref[:A_WIDTH, :], preferred_element_type=_F32)
          + jnp.dot(yb, w_out_ref[A_WIDTH:, :], preferred_element_type=_F32))
    o_ref[...] = x1
    h_ref[...] = _rms(x1, g_ffn_ref[...]).astype(_BF16)
    for c in range(0, D_FF, FF_CHUNK):
        f = jnp.maximum(jnp.dot(h_ref[...], w1_ref[:, c:c + FF_CHUNK],
                                preferred_element_type=_F32), 0.0)
        o_ref[...] += jnp.dot((f * f).astype(_BF16), w2_ref[c:c + FF_CHUNK, :],
                              preferred_element_type=_F32)
    o_ref[...] = _rms(o_ref[...], g_fin_ref[...])


def _const_spec(shape):
    nd = len(shape)
    return pl.BlockSpec(shape, lambda *_: (0,) * nd, pipeline_mode=pl.Buffered(1))


def _rope_tables(seq):
    inv = 1.0 / (ROPE_BASE ** (jnp.arange(0, ROPE_DIM, 2, dtype=_F32) / ROPE_DIM))
    ang = jnp.arange(seq, dtype=_F32)[:, None] * inv[None, :]
    cos, sin = jnp.cos(ang), jnp.sin(ang)
    zero = jnp.zeros_like(cos)
    pad = jnp.zeros((seq, LANES - ROPE_DIM), _F32)
    cos_t = jnp.concatenate([cos, cos, pad], axis=1)
    sin_lo = jnp.concatenate([-sin, zero, pad], axis=1)
    sin_hi = jnp.concatenate([zero, sin, pad], axis=1)
    return cos_t, sin_lo, sin_hi


def _attend(q, k, vt):
    bsz, _, k_tiles, _, _ = vt.shape
    seq = k_tiles * K_TILE
    q_tiles = seq // Q_TILE
    return pl.pallas_call(
        _attend_kernel,
        grid=(bsz, q_tiles),
        in_specs=[pl.BlockSpec((Q_TILE, B_HEADS * QK_PAD), lambda b, i: (b * q_tiles + i, 0)),
                  pl.BlockSpec((seq, B_HEADS * QK_PAD), lambda b, i: (b, 0)),
                  pl.BlockSpec((None, B_HEADS, k_tiles, VT_ROWS, K_TILE),
                               lambda b, i: (b, 0, 0, 0, 0))],
        out_specs=pl.BlockSpec((Q_TILE, B_WIDTH), lambda b, i: (b * q_tiles + i, 0)),
        out_shape=jax.ShapeDtypeStruct((bsz * seq, B_WIDTH), _F32),
        scratch_shapes=[pltpu.VMEM((B_HEADS, QK_PAD, Q_TILE), _BF16)],
        compiler_params=pltpu.CompilerParams(
            dimension_semantics=("arbitrary", "arbitrary"),
            vmem_limit_bytes=VMEM_LIMIT_BYTES),
        name="attend",
    )(q, k, vt)


def _trunk(x, p):
    bsz, seq, _ = x.shape
    n_tok = bsz * seq
    x2 = x.reshape(n_tok, D_MODEL)
    steps_per_seq = seq // MIX_ROWS
    cos_t, sin_lo, sin_hi = _rope_tables(seq)

    tok = lambda width: pl.BlockSpec((MIX_ROWS, width), lambda i: (i, 0))
    rope = pl.BlockSpec((MIX_ROWS, LANES), lambda i: (i % steps_per_seq, 0))
    params = pltpu.CompilerParams(dimension_semantics=("arbitrary",),
                                  vmem_limit_bytes=VMEM_LIMIT_BYTES)

    ya, q, k, vt = pl.pallas_call(
        _mix_in_kernel,
        grid=(n_tok // MIX_ROWS,),
        in_specs=[tok(D_MODEL), rope, rope, rope,
                  _const_spec((1, D_MODEL)), _const_spec((D_MODEL, IN_COLS_PAD)),
                  _const_spec((1, A_WIDTH)), _const_spec((A_HEADS, CHUNK, CHUNK)),
                  _const_spec((CHUNK, A_HEADS)), _const_spec((1, Q_RANK)),
                  _const_spec((Q_RANK, B_HEADS * QK_PAD)), _const_spec((1, KV_RANK)),
                  _const_spec((KV_RANK, B_HEADS * (NOPE_DIM + V_DIM))),
                  _const_spec((1, A_WIDTH))],
        out_specs=[tok(A_WIDTH), tok(B_HEADS * QK_PAD), tok(B_HEADS * QK_PAD),
                   pl.BlockSpec((None, B_HEADS, MIX_ROWS // K_TILE, VT_ROWS, K_TILE),
                                lambda i: (i // steps_per_seq, 0, i % steps_per_seq, 0, 0))],
        out_shape=[jax.ShapeDtypeStruct((n_tok, A_WIDTH), _BF16),
                   jax.ShapeDtypeStruct((n_tok, B_HEADS * QK_PAD), _BF16),
                   jax.ShapeDtypeStruct((n_tok, B_HEADS * QK_PAD), _BF16),
                   jax.ShapeDtypeStruct((bsz, B_HEADS, seq // K_TILE, VT_ROWS, K_TILE),
                                        _BF16)],
        compiler_params=params,
        name="mix_in",
    )(x2, cos_t, sin_lo, sin_hi, p["g_mix"], p["w_in"], p["g_sgu"], p["w_sp"], p["b_sp"],
      p["g_q"], p["w_uq"], p["g_kv"], p["w_ukv"], p["g_a"])

    yb = _attend(q, k, vt)

    ftok = lambda width: pl.BlockSpec((FFN_TILE, width), lambda i: (i, 0))
    out = pl.pallas_call(
        _ffn_out_kernel,
        grid=(n_tok // FFN_TILE,),
        in_specs=[ftok(D_MODEL), ftok(A_WIDTH), ftok(B_WIDTH), _const_spec((1, B_WIDTH)),
                  _const_spec((D_MODEL, D_MODEL)), _const_spec((1, D_MODEL)),
                  _const_spec((D_MODEL, D_FF)), _const_spec((D_FF, D_MODEL)),
                  _const_spec((1, D_MODEL))],
        out_specs=ftok(D_MODEL),
        out_shape=jax.ShapeDtypeStruct((n_tok, D_MODEL), _F32),
        scratch_shapes=[pltpu.VMEM((FFN_TILE, D_MODEL), _BF16)],
        compiler_params=params,
        name="ffn_out",
    )(x2, ya, yb, p["g_b"], p["w_out"], p["g_ffn"], p["w_ff1"], p["w_ff2"], p["g_fin"])
    return out.reshape(bsz, seq, D_MODEL)


def _prepare_params(norm_mix, w_in, sgu_norm, w_spatial, b_spatial, q_norm, w_uq, kv_norm,
                    w_ukv, out_norm_a, out_norm_b, w_out, norm_ffn, w_ff1, w_ff2, norm_final):
    w_in_p = jnp.pad(w_in[0], ((0, 0), (0, IN_COLS_PAD - IN_COLS))).astype(_BF16)
    uq = w_uq[0].reshape(Q_RANK, B_HEADS, QK_DIM)
    uq = jnp.pad(uq, ((0, 0), (0, 0), (0, QK_PAD - QK_DIM)))
    row = lambda g: g.reshape(1, -1).astype(_F32)
    return dict(
        g_mix=row(norm_mix[0]), w_in=w_in_p, g_sgu=row(sgu_norm[0]),
        w_sp=w_spatial[0].astype(_BF16), b_sp=b_spatial[0].T.astype(_F32),
        g_q=row(q_norm[0]), w_uq=uq.reshape(Q_RANK, B_HEADS * QK_PAD).astype(_BF16),
        g_kv=row(kv_norm[0]), w_ukv=w_ukv[0].astype(_BF16), g_a=row(out_norm_a[0]),
        g_b=row(out_norm_b[0]), w_out=w_out[0].astype(_BF16), g_ffn=row(norm_ffn[0]),
        w_ff1=w_ff1[0].astype(_BF16), w_ff2=w_ff2[0].astype(_BF16), g_fin=row(norm_final))


def kernel(x_prompt, x_sample, norm_mix, w_in, sgu_norm, w_spatial, b_spatial, q_norm, w_uq,
           kv_norm, w_ukv, out_norm_a, out_norm_b, w_out, norm_ffn, w_ff1, w_ff2, norm_final):
    assert norm_mix.shape[0] == 1, "single-layer trunk"
    p = _prepare_params(norm_mix, w_in, sgu_norm, w_spatial, b_spatial, q_norm, w_uq, kv_norm,
                        w_ukv, out_norm_a, out_norm_b, w_out, norm_ffn, w_ff1, w_ff2,
                        norm_final)
    return (_trunk(x_prompt, p), _trunk(x_sample, p))
```

```python
import math

import jax
import jax.numpy as jnp
from jax.experimental import pallas as pl
from jax.experimental.pallas import tpu as pltpu

LANES = 128
D_MODEL = 1024
A_WIDTH = 512
A_HEADS = 4
A_HEAD_DIM = A_WIDTH // A_HEADS
CHUNK = 128
B_HEADS = 4
NOPE_DIM = 128
ROPE_DIM = 64
V_DIM = 128
QK_DIM = NOPE_DIM + ROPE_DIM
QK_PAD = 2 * LANES
VT_ROWS = V_DIM + 16
B_WIDTH = B_HEADS * V_DIM
Q_RANK = 384
KV_RANK = 256
IN_COLS = 2 * A_WIDTH + Q_RANK + KV_RANK + ROPE_DIM
IN_COLS_PAD = 2 * A_WIDTH + Q_RANK + KV_RANK + LANES
D_FF = 4 * D_MODEL
ROPE_BASE = 10000.0
EPS = 1e-6
ATTN_SCALE = QK_DIM ** -0.5
Q_SCALE = ATTN_SCALE * math.log2(math.e)
VMEM_LIMIT_BYTES = 56 * 1024 * 1024

MIX_ROWS = 1024
MIX_SUB = 256
FFN_TILE = 1024
FFN_SUB = 512
Q_TILE = 512
K_TILE = 512
K_SUB = 256
K_AHEAD = 2
HEAD_GROUP = 4
LAG_LIMIT = 64.0
assert MIX_ROWS % K_TILE == 0 and K_TILE % MIX_SUB == 0
FF_CHUNK = 1024

_BF16 = jnp.bfloat16
_F32 = jnp.float32


def _rms(x, gain):
    return x * jax.lax.rsqrt(jnp.mean(x * x, axis=-1, keepdims=True) + EPS) * gain


def _gelu_tanh(x):
    c = math.sqrt(2.0 / math.pi)
    return x * (0.5 + 0.5 * jnp.tanh(x * (c + (c * 0.044715) * (x * x))))


def _rope_padded(x, cos_t, sin_lo, sin_hi):
    half = ROPE_DIM // 2
    return (x * cos_t + pltpu.roll(x, LANES - half, 1) * sin_lo
            + pltpu.roll(x, half, 1) * sin_hi)


def _mix_in_kernel(x_ref, cos_ref, slo_ref, shi_ref, g_mix_ref, w_in_ref, g_sgu_ref,
                   w_sp_ref, b_sp_ref, g_q_ref, w_uq_ref, g_kv_ref, w_ukv_ref, g_a_ref,
                   ya_ref, q_ref, k_ref, vt_ref):
    starts = range(0, MIX_ROWS, MIX_SUB)
    zs = [jnp.dot(_rms(x_ref[r0:r0 + MIX_SUB, :], g_mix_ref[...]).astype(_BF16), w_in_ref[...],
                  preferred_element_type=_F32) for r0 in starts]
    for z, r0 in zip(zs, starts):
        rs = slice(r0, r0 + MIX_SUB)
        kt, off = divmod(r0, K_TILE)
        _mix_in_rows(z, cos_ref[rs, :], slo_ref[rs, :], shi_ref[rs, :], g_sgu_ref, w_sp_ref,
                     b_sp_ref, g_q_ref, w_uq_ref, g_kv_ref, w_ukv_ref, g_a_ref,
                     ya_ref.at[rs, :], q_ref.at[rs, :], k_ref.at[rs, :],
                     vt_ref.at[:, kt, :, off:off + MIX_SUB])


def _mix_in_rows(z, cos_t, sin_lo, sin_hi, g_sgu_ref, w_sp_ref, b_sp_ref, g_q_ref, w_uq_ref,
                 g_kv_ref, w_ukv_ref, g_a_ref, ya_ref, q_ref, k_ref, vt_ref):
    rows = z.shape[0]
    u = _gelu_tanh(z[:, :A_WIDTH])
    v = _gelu_tanh(z[:, A_WIDTH:2 * A_WIDTH])

    ya_cols = []
    for hd in range(A_HEADS):
        cols = slice(hd * A_HEAD_DIM, (hd + 1) * A_HEAD_DIM)
        vn = _rms(v[:, cols], g_sgu_ref[:, cols]).astype(_BF16)
        bias = jnp.broadcast_to(b_sp_ref[:, hd:hd + 1], (CHUNK, A_HEAD_DIM))
        parts = []
        for c in range(rows // CHUNK):
            s = jnp.dot(w_sp_ref[hd], vn[c * CHUNK:(c + 1) * CHUNK],
                        preferred_element_type=_F32) + bias
            parts.append(s)
        ya_cols.append(u[:, cols] * jnp.concatenate(parts, axis=0))
    ya = jnp.concatenate(ya_cols, axis=1)
    ya_ref[...] = _rms(ya, g_a_ref[...]).astype(_BF16)

    o = 2 * A_WIDTH
    cq = _rms(z[:, o:o + Q_RANK], g_q_ref[...]).astype(_BF16)
    q = jnp.dot(cq, w_uq_ref[...], preferred_element_type=_F32)
    for hd in range(B_HEADS):
        base = hd * QK_PAD
        q_ref[:, base:base + LANES] = (q[:, base:base + LANES] * Q_SCALE).astype(_BF16)
        pe = _rope_padded(q[:, base + LANES:base + QK_PAD], cos_t, sin_lo, sin_hi)
        q_ref[:, base + LANES:base + QK_PAD] = (pe * Q_SCALE).astype(_BF16)

    o += Q_RANK
    ckv = _rms(z[:, o:o + KV_RANK], g_kv_ref[...]).astype(_BF16)
    kv = jnp.dot(ckv, w_ukv_ref[...], preferred_element_type=_F32)
    o += KV_RANK
    k_pe = _rope_padded(z[:, o:o + LANES], cos_t, sin_lo, sin_hi).astype(_BF16)
    for hd in range(B_HEADS):
        src = hd * (NOPE_DIM + V_DIM)
        k_ref[:, hd * QK_PAD:hd * QK_PAD + LANES] = kv[:, src:src + NOPE_DIM].astype(_BF16)
        k_ref[:, hd * QK_PAD + LANES:(hd + 1) * QK_PAD] = k_pe
        vt_ref[hd, :V_DIM, :] = kv[:, src + NOPE_DIM:src + NOPE_DIM + V_DIM].T.astype(_BF16)
        vt_ref[hd, V_DIM:, :] = jnp.ones((VT_ROWS - V_DIM, rows), _BF16)


def _scores_t(q_ref, k_ref, hd, row0, n_rows):
    cols = slice(hd * QK_PAD, (hd + 1) * QK_PAD)
    if not isinstance(row0, int):
        row0 = pl.multiple_of(row0, n_rows)
    return jax.lax.dot_general(k_ref[pl.ds(row0, n_rows), cols], q_ref[:, cols],
                               (((1,), (1,)), ((), ())), preferred_element_type=_F32)


def _store_head(o_ref, hd, acc):
    o_ref[:, hd * V_DIM:(hd + 1) * V_DIM] = (
        acc[:V_DIM] / acc[V_DIM:V_DIM + 1]).T.astype(o_ref.dtype)


def _attend_kernel(q_ref, k_ref, vt_ref, o_ref, qt_ref):
    n_tiles = k_ref.shape[0] // K_TILE
    n_sub = K_TILE // K_SUB
    units = [(hd, t, u) for g in range(0, B_HEADS, HEAD_GROUP) for t in range(n_tiles)
             for u in range(n_sub) for hd in range(g, g + HEAD_GROUP)]

    for hd in range(B_HEADS):
        qt_ref[hd] = q_ref[:, hd * QK_PAD:(hd + 1) * QK_PAD].T

    def scores_u(hd, t, u):
        r0 = t * K_TILE + u * K_SUB
        return jnp.dot(k_ref[r0:r0 + K_SUB, hd * QK_PAD:(hd + 1) * QK_PAD], qt_ref[hd],
                       preferred_element_type=_F32)

    peak = jnp.zeros((1, Q_TILE), _F32)
    shift, acc = {}, {}
    ahead = [scores_u(*un) for un in units[:K_AHEAD]]
    for i, (hd, t, u) in enumerate(units):
        st = ahead.pop(0)
        if i + K_AHEAD < len(units):
            ahead.append(scores_u(*units[i + K_AHEAD]))
        first = t == 0 and u == 0
        if first:
            shift[hd] = jnp.max(st, axis=0, keepdims=True)
        p = jnp.exp2(st - shift[hd]).astype(_BF16)
        if not first:
            peak = jnp.maximum(peak, jnp.max(p, axis=0, keepdims=True).astype(_F32))
        pv = jnp.dot(vt_ref[hd, t, :, u * K_SUB:(u + 1) * K_SUB], p,
                     preferred_element_type=_F32)
        acc[hd] = pv if first else acc[hd] + pv
        if t == n_tiles - 1 and u == n_sub - 1:
            _store_head(o_ref, hd, acc.pop(hd))

    @pl.when(jnp.max(peak) > 2.0 ** LAG_LIMIT)
    def _():
        for hd in range(B_HEADS):
            def tile_step(t, carry, hd=hd):
                m, acc = carry
                st = _scores_t(q_ref, k_ref, hd, t * K_TILE, K_TILE)
                m_new = jnp.maximum(m, jnp.max(st, axis=0, keepdims=True))
                acc = jnp.exp2(m - m_new) * acc + jnp.dot(
                    vt_ref[hd, t], jnp.exp2(st - m_new).astype(_BF16),
                    preferred_element_type=_F32)
                return m_new, acc

            init = (jnp.full((1, Q_TILE), -jnp.inf, _F32), jnp.zeros((VT_ROWS, Q_TILE), _F32))
            _store_head(o_ref, hd, jax.lax.fori_loop(0, n_tiles, tile_step, init)[1])


def _ffn_out_kernel(x_ref, ya_ref, yb_ref, g_b_ref, w_out_ref, g_ffn_ref, w1_ref, w2_ref,
                    g_fin_ref, o_ref, h_ref):
    subs = [slice(r0, r0 + FFN_SUB) for r0 in range(0, FFN_TILE, FFN_SUB)]
    for rs in subs:
        yb = _rms(yb_ref[rs, :], g_b_ref[...]).astype(_BF16)
        x1 = (x_ref[rs, :]
              + jnp.dot(ya_ref[rs, :], w_out_ref[:A_WIDTH, :], preferred_element_type=_F32)
              + jnp.dot(yb, w_out_ref[A_WIDTH:, :], preferred_element_type=_F32))
        o_ref[rs, :] = x1
        h_ref[rs, :] = _rms(x1, g_ffn_ref[...]).astype(_BF16)
    for rs in subs:
        for c in range(0, D_FF, FF_CHUNK):
            f = jnp.maximum(jnp.dot(h_ref[rs, :], w1_ref[:, c:c + FF_CHUNK],
                                    preferred_element_type=_F32), 0.0)
            o_ref[rs, :] += jnp.dot((f * f).astype(_BF16), w2_ref[c:c + FF_CHUNK, :],
                                    preferred_element_type=_F32)
        o_ref[rs, :] = _rms(o_ref[rs, :], g_fin_ref[...])


def _const_spec(shape):
    nd = len(shape)
    return pl.BlockSpec(shape, lambda *_: (0,) * nd, pipeline_mode=pl.Buffered(1))


def _rope_tables(seq):
    inv = 1.0 / (ROPE_BASE ** (jnp.arange(0, ROPE_DIM, 2, dtype=_F32) / ROPE_DIM))
    ang = jnp.arange(seq, dtype=_F32)[:, None] * inv[None, :]
    cos, sin = jnp.cos(ang), jnp.sin(ang)
    zero = jnp.zeros_like(cos)
    pad = jnp.zeros((seq, LANES - ROPE_DIM), _F32)
    cos_t = jnp.concatenate([cos, cos, pad], axis=1)
    sin_lo = jnp.concatenate([-sin, zero, pad], axis=1)
    sin_hi = jnp.concatenate([zero, sin, pad], axis=1)
    return cos_t, sin_lo, sin_hi


def _attend(q, k, vt):
    bsz, _, k_tiles, _, _ = vt.shape
    seq = k_tiles * K_TILE
    q_tiles = seq // Q_TILE
    return pl.pallas_call(
        _attend_kernel,
        grid=(bsz, q_tiles),
        in_specs=[pl.BlockSpec((Q_TILE, B_HEADS * QK_PAD), lambda b, i: (b * q_tiles + i, 0)),
                  pl.BlockSpec((seq, B_HEADS * QK_PAD), lambda b, i: (b, 0)),
                  pl.BlockSpec((None, B_HEADS, k_tiles, VT_ROWS, K_TILE),
                               lambda b, i: (b, 0, 0, 0, 0))],
        out_specs=pl.BlockSpec((Q_TILE, B_WIDTH), lambda b, i: (b * q_tiles + i, 0)),
        out_shape=jax.ShapeDtypeStruct((bsz * seq, B_WIDTH), _F32),
        scratch_shapes=[pltpu.VMEM((B_HEADS, QK_PAD, Q_TILE), _BF16)],
        compiler_params=pltpu.CompilerParams(
            dimension_semantics=("arbitrary", "arbitrary"),
            vmem_limit_bytes=VMEM_LIMIT_BYTES),
        name="attend",
    )(q, k, vt)


def _trunk(x, p):
    bsz, seq, _ = x.shape
    n_tok = bsz * seq
    x2 = x.reshape(n_tok, D_MODEL)
    steps_per_seq = seq // MIX_ROWS
    cos_t, sin_lo, sin_hi = _rope_tables(seq)

    tok = lambda width: pl.BlockSpec((MIX_ROWS, width), lambda i: (i, 0))
    rope = pl.BlockSpec((MIX_ROWS, LANES), lambda i: (i % steps_per_seq, 0))
    params = pltpu.CompilerParams(dimension_semantics=("arbitrary",),
                                  vmem_limit_bytes=VMEM_LIMIT_BYTES)

    ya, q, k, vt = pl.pallas_call(
        _mix_in_kernel,
        grid=(n_tok // MIX_ROWS,),
        in_specs=[tok(D_MODEL), rope, rope, rope,
                  _const_spec((1, D_MODEL)), _const_spec((D_MODEL, IN_COLS_PAD)),
                  _const_spec((1, A_WIDTH)), _const_spec((A_HEADS, CHUNK, CHUNK)),
                  _const_spec((CHUNK, A_HEADS)), _const_spec((1, Q_RANK)),
                  _const_spec((Q_RANK, B_HEADS * QK_PAD)), _const_spec((1, KV_RANK)),
                  _const_spec((KV_RANK, B_HEADS * (NOPE_DIM + V_DIM))),
                  _const_spec((1, A_WIDTH))],
        out_specs=[tok(A_WIDTH), tok(B_HEADS * QK_PAD), tok(B_HEADS * QK_PAD),
                   pl.BlockSpec((None, B_HEADS, MIX_ROWS // K_TILE, VT_ROWS, K_TILE),
                                lambda i: (i // steps_per_seq, 0, i % steps_per_seq, 0, 0))],
        out_shape=[jax.ShapeDtypeStruct((n_tok, A_WIDTH), _BF16),
                   jax.ShapeDtypeStruct((n_tok, B_HEADS * QK_PAD), _BF16),
                   jax.ShapeDtypeStruct((n_tok, B_HEADS * QK_PAD), _BF16),
                   jax.ShapeDtypeStruct((bsz, B_HEADS, seq // K_TILE, VT_ROWS, K_TILE),
                                        _BF16)],
        compiler_params=params,
        name="mix_in",
    )(x2, cos_t, sin_lo, sin_hi, p["g_mix"], p["w_in"], p["g_sgu"], p["w_sp"], p["b_sp"],
      p["g_q"], p["w_uq"], p["g_kv"], p["w_ukv"], p["g_a"])

    yb = _attend(q, k, vt)

    ftok = lambda width: pl.BlockSpec((FFN_TILE, width), lambda i: (i, 0))
    out = pl.pallas_call(
        _ffn_out_kernel,
        grid=(n_tok // FFN_TILE,),
        in_specs=[ftok(D_MODEL), ftok(A_WIDTH), ftok(B_WIDTH), _const_spec((1, B_WIDTH)),
                  _const_spec((D_MODEL, D_MODEL)), _const_spec((1, D_MODEL)),
                  _const_spec((D_MODEL, D_FF)), _const_spec((D_FF, D_MODEL)),
                  _const_spec((1, D_MODEL))],
        out_specs=ftok(D_MODEL),
        out_shape=jax.ShapeDtypeStruct((n_tok, D_MODEL), _F32),
        scratch_shapes=[pltpu.VMEM((FFN_TILE, D_MODEL), _BF16)],
        compiler_params=params,
        name="ffn_out",
    )(x2, ya, yb, p["g_b"], p["w_out"], p["g_ffn"], p["w_ff1"], p["w_ff2"], p["g_fin"])
    return out.reshape(bsz, seq, D_MODEL)


def _prepare_params(norm_mix, w_in, sgu_norm, w_spatial, b_spatial, q_norm, w_uq, kv_norm,
                    w_ukv, out_norm_a, out_norm_b, w_out, norm_ffn, w_ff1, w_ff2, norm_final):
    w_in_p = jnp.pad(w_in[0], ((0, 0), (0, IN_COLS_PAD - IN_COLS))).astype(_BF16)
    uq = w_uq[0].reshape(Q_RANK, B_HEADS, QK_DIM)
    uq = jnp.pad(uq, ((0, 0), (0, 0), (0, QK_PAD - QK_DIM)))
    row = lambda g: g.reshape(1, -1).astype(_F32)
    return dict(
        g_mix=row(norm_mix[0]), w_in=w_in_p, g_sgu=row(sgu_norm[0]),
        w_sp=w_spatial[0].astype(_BF16), b_sp=b_spatial[0].T.astype(_F32),
        g_q=row(q_norm[0]), w_uq=uq.reshape(Q_RANK, B_HEADS * QK_PAD).astype(_BF16),
        g_kv=row(kv_norm[0]), w_ukv=w_ukv[0].astype(_BF16), g_a=row(out_norm_a[0]),
        g_b=row(out_norm_b[0]), w_out=w_out[0].astype(_BF16), g_ffn=row(norm_ffn[0]),
        w_ff1=w_ff1[0].astype(_BF16), w_ff2=w_ff2[0].astype(_BF16), g_fin=row(norm_final))


def kernel(x_prompt, x_sample, norm_mix, w_in, sgu_norm, w_spatial, b_spatial, q_norm, w_uq,
           kv_norm, w_ukv, out_norm_a, out_norm_b, w_out, norm_ffn, w_ff1, w_ff2, norm_final):
    assert norm_mix.shape[0] == 1, "single-layer trunk"
    p = _prepare_params(norm_mix, w_in, sgu_norm, w_spatial, b_spatial, q_norm, w_uq, kv_norm,
                        w_ukv, out_norm_a, out_norm_b, w_out, norm_ffn, w_ff1, w_ff2,
                        norm_final)
    return (_trunk(x_prompt, p), _trunk(x_sample, p))
```

```python
import math

import jax
import jax.numpy as jnp
from jax.experimental import pallas as pl
from jax.experimental.pallas import tpu as pltpu

LANES = 128
D_MODEL = 1024
A_WIDTH = 512
A_HEADS = 4
A_HEAD_DIM = A_WIDTH // A_HEADS
CHUNK = 128
B_HEADS = 4
NOPE_DIM = 128
ROPE_DIM = 64
V_DIM = 128
QK_DIM = NOPE_DIM + ROPE_DIM
QK_PAD = 2 * LANES
VT_ROWS = V_DIM + 16
B_WIDTH = B_HEADS * V_DIM
Q_RANK = 384
KV_RANK = 256
IN_COLS = 2 * A_WIDTH + Q_RANK + KV_RANK + ROPE_DIM
IN_COLS_PAD = 2 * A_WIDTH + Q_RANK + KV_RANK + LANES
D_FF = 4 * D_MODEL
ROPE_BASE = 10000.0
EPS = 1e-6
ATTN_SCALE = QK_DIM ** -0.5
Q_SCALE = ATTN_SCALE * math.log2(math.e)
VMEM_LIMIT_BYTES = 56 * 1024 * 1024

MIX_ROWS = 1024
MIX_SUB = 256
FFN_TILE = 1024
FFN_SUB = 512
Q_TILE = 512
Q_STREAMS = 2
K_TILE = 512
K_SUB = 256
K_AHEAD = 2
HEAD_GROUP = 4
LAG_LIMIT = 64.0
assert MIX_ROWS % K_TILE == 0 and K_TILE % MIX_SUB == 0
FF_CHUNK = 1024

_BF16 = jnp.bfloat16
_F32 = jnp.float32


def _rms(x, gain):
    return x * jax.lax.rsqrt(jnp.mean(x * x, axis=-1, keepdims=True) + EPS) * gain


def _gelu_tanh(x):
    c = math.sqrt(2.0 / math.pi)
    return x * (0.5 + 0.5 * jnp.tanh(x * (c + (c * 0.044715) * (x * x))))


def _rope_padded(x, cos_t, sin_lo, sin_hi):
    half = ROPE_DIM // 2
    return (x * cos_t + pltpu.roll(x, LANES - half, 1) * sin_lo
            + pltpu.roll(x, half, 1) * sin_hi)


def _mix_in_kernel(x_ref, cos_ref, slo_ref, shi_ref, g_mix_ref, w_in_ref, g_sgu_ref,
                   w_sp_ref, b_sp_ref, g_q_ref, w_uq_ref, g_kv_ref, w_ukv_ref, g_a_ref,
                   ya_ref, q_ref, k_ref, vt_ref):
    starts = range(0, MIX_ROWS, MIX_SUB)
    zs = [jnp.dot(_rms(x_ref[r0:r0 + MIX_SUB, :], g_mix_ref[...]).astype(_BF16), w_in_ref[...],
                  preferred_element_type=_F32) for r0 in starts]
    for z, r0 in zip(zs, starts):
        rs = slice(r0, r0 + MIX_SUB)
        kt, off = divmod(r0, K_TILE)
        _mix_in_rows(z, cos_ref[rs, :], slo_ref[rs, :], shi_ref[rs, :], g_sgu_ref, w_sp_ref,
                     b_sp_ref, g_q_ref, w_uq_ref, g_kv_ref, w_ukv_ref, g_a_ref,
                     ya_ref.at[rs, :], q_ref.at[rs, :], k_ref.at[rs, :],
                     vt_ref.at[:, kt, :, off:off + MIX_SUB])


def _mix_in_rows(z, cos_t, sin_lo, sin_hi, g_sgu_ref, w_sp_ref, b_sp_ref, g_q_ref, w_uq_ref,
                 g_kv_ref, w_ukv_ref, g_a_ref, ya_ref, q_ref, k_ref, vt_ref):
    rows = z.shape[0]
    u = _gelu_tanh(z[:, :A_WIDTH])
    v = _gelu_tanh(z[:, A_WIDTH:2 * A_WIDTH])

    ya_cols = []
    for hd in range(A_HEADS):
        cols = slice(hd * A_HEAD_DIM, (hd + 1) * A_HEAD_DIM)
        vn = _rms(v[:, cols], g_sgu_ref[:, cols]).astype(_BF16)
        bias = jnp.broadcast_to(b_sp_ref[:, hd:hd + 1], (CHUNK, A_HEAD_DIM))
        parts = []
        for c in range(rows // CHUNK):
            s = jnp.dot(w_sp_ref[hd], vn[c * CHUNK:(c + 1) * CHUNK],
                        preferred_element_type=_F32) + bias
            parts.append(s)
        ya_cols.append(u[:, cols] * jnp.concatenate(parts, axis=0))
    ya = jnp.concatenate(ya_cols, axis=1)
    ya_ref[...] = _rms(ya, g_a_ref[...]).astype(_BF16)

    o = 2 * A_WIDTH
    cq = _rms(z[:, o:o + Q_RANK], g_q_ref[...]).astype(_BF16)
    q = jnp.dot(cq, w_uq_ref[...], preferred_element_type=_F32)
    for hd in range(B_HEADS):
        base = hd * QK_PAD
        q_ref[:, base:base + LANES] = (q[:, base:base + LANES] * Q_SCALE).astype(_BF16)
        pe = _rope_padded(q[:, base + LANES:base + QK_PAD], cos_t, sin_lo, sin_hi)
        q_ref[:, base + LANES:base + QK_PAD] = (pe * Q_SCALE).astype(_BF16)

    o += Q_RANK
    ckv = _rms(z[:, o:o + KV_RANK], g_kv_ref[...]).astype(_BF16)
    kv = jnp.dot(ckv, w_ukv_ref[...], preferred_element_type=_F32)
    o += KV_RANK
    k_pe = _rope_padded(z[:, o:o + LANES], cos_t, sin_lo, sin_hi).astype(_BF16)
    for hd in range(B_HEADS):
        src = hd * (NOPE_DIM + V_DIM)
        k_ref[:, hd * QK_PAD:hd * QK_PAD + LANES] = kv[:, src:src + NOPE_DIM].astype(_BF16)
        k_ref[:, hd * QK_PAD + LANES:(hd + 1) * QK_PAD] = k_pe
        vt_ref[hd, :V_DIM, :] = kv[:, src + NOPE_DIM:src + NOPE_DIM + V_DIM].T.astype(_BF16)
        vt_ref[hd, V_DIM:, :] = jnp.ones((VT_ROWS - V_DIM, rows), _BF16)


def _scores_t(qt, k_ref, hd, row0, n_rows):
    if not isinstance(row0, int):
        row0 = pl.multiple_of(row0, n_rows)
    return jnp.dot(k_ref[pl.ds(row0, n_rows), hd * QK_PAD:(hd + 1) * QK_PAD], qt,
                   preferred_element_type=_F32)


def _store_head(o_ref, s, hd, acc):
    o_ref[s * Q_TILE:(s + 1) * Q_TILE, hd * V_DIM:(hd + 1) * V_DIM] = (
        acc[:V_DIM] / acc[V_DIM:V_DIM + 1]).T.astype(o_ref.dtype)


def _attend_kernel(q_ref, k_ref, vt_ref, o_ref, qt_ref):
    n_tiles = k_ref.shape[0] // K_TILE
    n_sub = K_TILE // K_SUB
    units = [(s, hd, t, u) for s in range(Q_STREAMS) for g in range(0, B_HEADS, HEAD_GROUP)
             for t in range(n_tiles) for u in range(n_sub) for hd in range(g, g + HEAD_GROUP)]

    for s in range(Q_STREAMS):
        for hd in range(B_HEADS):
            qt_ref[s, hd] = q_ref[s * Q_TILE:(s + 1) * Q_TILE, hd * QK_PAD:(hd + 1) * QK_PAD].T

    def scores_u(s, hd, t, u):
        return _scores_t(qt_ref[s, hd], k_ref, hd, t * K_TILE + u * K_SUB, K_SUB)

    peak = jnp.zeros((1, Q_TILE), _F32)
    shift, acc = {}, {}
    ahead = [scores_u(*un) for un in units[:K_AHEAD]]
    for i, (s, hd, t, u) in enumerate(units):
        st = ahead.pop(0)
        if i + K_AHEAD < len(units):
            ahead.append(scores_u(*units[i + K_AHEAD]))
        first = t == 0 and u == 0
        if first:
            shift[s, hd] = jnp.max(st, axis=0, keepdims=True)
        p = jnp.exp2(st - shift[s, hd]).astype(_BF16)
        if not first:
            peak = jnp.maximum(peak, jnp.max(p, axis=0, keepdims=True).astype(_F32))
        pv = jnp.dot(vt_ref[hd, t, :, u * K_SUB:(u + 1) * K_SUB], p,
                     preferred_element_type=_F32)
        acc[s, hd] = pv if first else acc[s, hd] + pv
        if t == n_tiles - 1 and u == n_sub - 1:
            _store_head(o_ref, s, hd, acc.pop((s, hd)))

    @pl.when(jnp.max(peak) > 2.0 ** LAG_LIMIT)
    def _():
        for s in range(Q_STREAMS):
            for hd in range(B_HEADS):
                def tile_step(t, carry, s=s, hd=hd):
                    m, acc = carry
                    st = _scores_t(qt_ref[s, hd], k_ref, hd, t * K_TILE, K_TILE)
                    m_new = jnp.maximum(m, jnp.max(st, axis=0, keepdims=True))
                    acc = jnp.exp2(m - m_new) * acc + jnp.dot(
                        vt_ref[hd, t], jnp.exp2(st - m_new).astype(_BF16),
                        preferred_element_type=_F32)
                    return m_new, acc

                init = (jnp.full((1, Q_TILE), -jnp.inf, _F32),
                        jnp.zeros((VT_ROWS, Q_TILE), _F32))
                _store_head(o_ref, s, hd, jax.lax.fori_loop(0, n_tiles, tile_step, init)[1])


def _ffn_out_kernel(x_ref, ya_ref, yb_ref, g_b_ref, w_out_ref, g_ffn_ref, w1_ref, w2_ref,
                    g_fin_ref, o_ref, h_ref):
    subs = [slice(r0, r0 + FFN_SUB) for r0 in range(0, FFN_TILE, FFN_SUB)]
    for rs in subs:
        yb = _rms(yb_ref[rs, :], g_b_ref[...]).astype(_BF16)
        x1 = (x_ref[rs, :]
              + jnp.dot(ya_ref[rs, :], w_out_ref[:A_WIDTH, :], preferred_element_type=_F32)
              + jnp.dot(yb, w_out_ref[A_WIDTH:, :], preferred_element_type=_F32))
        o_ref[rs, :] = x1
        h_ref[rs, :] = _rms(x1, g_ffn_ref[...]).astype(_BF16)
    for rs in subs:
        for c in range(0, D_FF, FF_CHUNK):
            f = jnp.maximum(jnp.dot(h_ref[rs, :], w1_ref[:, c:c + FF_CHUNK],
                                    preferred_element_type=_F32), 0.0)
            o_ref[rs, :] += jnp.dot((f * f).astype(_BF16), w2_ref[c:c + FF_CHUNK, :],
                                    preferred_element_type=_F32)
        o_ref[rs, :] = _rms(o_ref[rs, :], g_fin_ref[...])


def _const_spec(shape):
    nd = len(shape)
    return pl.BlockSpec(shape, lambda *_: (0,) * nd, pipeline_mode=pl.Buffered(1))


def _rope_tables(seq):
    inv = 1.0 / (ROPE_BASE ** (jnp.arange(0, ROPE_DIM, 2, dtype=_F32) / ROPE_DIM))
    ang = jnp.arange(seq, dtype=_F32)[:, None] * inv[None, :]
    cos, sin = jnp.cos(ang), jnp.sin(ang)
    zero = jnp.zeros_like(cos)
    pad = jnp.zeros((seq, LANES - ROPE_DIM), _F32)
    cos_t = jnp.concatenate([cos, cos, pad], axis=1)
    sin_lo = jnp.concatenate([-sin, zero, pad], axis=1)
    sin_hi = jnp.concatenate([zero, sin, pad], axis=1)
    return cos_t, sin_lo, sin_hi


def _attend(q, k, vt):
    bsz, _, k_tiles, _, _ = vt.shape
    seq = k_tiles * K_TILE
    q_rows = Q_STREAMS * Q_TILE
    q_tiles = seq // q_rows
    return pl.pallas_call(
        _attend_kernel,
        grid=(bsz, q_tiles),
        in_specs=[pl.BlockSpec((q_rows, B_HEADS * QK_PAD), lambda b, i: (b * q_tiles + i, 0)),
                  pl.BlockSpec((seq, B_HEADS * QK_PAD), lambda b, i: (b, 0)),
                  pl.BlockSpec((None, B_HEADS, k_tiles, VT_ROWS, K_TILE),
                               lambda b, i: (b, 0, 0, 0, 0))],
        out_specs=pl.BlockSpec((q_rows, B_WIDTH), lambda b, i: (b * q_tiles + i, 0)),
        out_shape=jax.ShapeDtypeStruct((bsz * seq, B_WIDTH), _F32),
        scratch_shapes=[pltpu.VMEM((Q_STREAMS, B_HEADS, QK_PAD, Q_TILE), _BF16)],
        compiler_params=pltpu.CompilerParams(
            dimension_semantics=("arbitrary", "arbitrary"),
            vmem_limit_bytes=VMEM_LIMIT_BYTES),
        name="attend",
    )(q, k, vt)


def _trunk(x, p):
    bsz, seq, _ = x.shape
    n_tok = bsz * seq
    x2 = x.reshape(n_tok, D_MODEL)
    steps_per_seq = seq // MIX_ROWS
    cos_t, sin_lo, sin_hi = _rope_tables(seq)

    tok = lambda width: pl.BlockSpec((MIX_ROWS, width), lambda i: (i, 0))
    rope = pl.BlockSpec((MIX_ROWS, LANES), lambda i: (i % steps_per_seq, 0))
    params = pltpu.CompilerParams(dimension_semantics=("arbitrary",),
                                  vmem_limit_bytes=VMEM_LIMIT_BYTES)

    ya, q, k, vt = pl.pallas_call(
        _mix_in_kernel,
        grid=(n_tok // MIX_ROWS,),
        in_specs=[tok(D_MODEL), rope, rope, rope,
                  _const_spec((1, D_MODEL)), _const_spec((D_MODEL, IN_COLS_PAD)),
                  _const_spec((1, A_WIDTH)), _const_spec((A_HEADS, CHUNK, CHUNK)),
                  _const_spec((CHUNK, A_HEADS)), _const_spec((1, Q_RANK)),
                  _const_spec((Q_RANK, B_HEADS * QK_PAD)), _const_spec((1, KV_RANK)),
                  _const_spec((KV_RANK, B_HEADS * (NOPE_DIM + V_DIM))),
                  _const_spec((1, A_WIDTH))],
        out_specs=[tok(A_WIDTH), tok(B_HEADS * QK_PAD), tok(B_HEADS * QK_PAD),
                   pl.BlockSpec((None, B_HEADS, MIX_ROWS // K_TILE, VT_ROWS, K_TILE),
                                lambda i: (i // steps_per_seq, 0, i % steps_per_seq, 0, 0))],
        out_shape=[jax.ShapeDtypeStruct((n_tok, A_WIDTH), _BF16),
                   jax.ShapeDtypeStruct((n_tok, B_HEADS * QK_PAD), _BF16),
                   jax.ShapeDtypeStruct((n_tok, B_HEADS * QK_PAD), _BF16),
                   jax.ShapeDtypeStruct((bsz, B_HEADS, seq // K_TILE, VT_ROWS, K_TILE),
                                        _BF16)],
        compiler_params=params,
        name="mix_in",
    )(x2, cos_t, sin_lo, sin_hi, p["g_mix"], p["w_in"], p["g_sgu"], p["w_sp"], p["b_sp"],
      p["g_q"], p["w_uq"], p["g_kv"], p["w_ukv"], p["g_a"])

    yb = _attend(q, k, vt)

    ftok = lambda width: pl.BlockSpec((FFN_TILE, width), lambda i: (i, 0))
    out = pl.pallas_call(
        _ffn_out_kernel,
        grid=(n_tok // FFN_TILE,),
        in_specs=[ftok(D_MODEL), ftok(A_WIDTH), ftok(B_WIDTH), _const_spec((1, B_WIDTH)),
                  _const_spec((D_MODEL, D_MODEL)), _const_spec((1, D_MODEL)),
                  _const_spec((D_MODEL, D_FF)), _const_spec((D_FF, D_MODEL)),
                  _const_spec((1, D_MODEL))],
        out_specs=ftok(D_MODEL),
        out_shape=jax.ShapeDtypeStruct((n_tok, D_MODEL), _F32),
        scratch_shapes=[pltpu.VMEM((FFN_TILE, D_MODEL), _BF16)],
        compiler_params=params,
        name="ffn_out",
    )(x2, ya, yb, p["g_b"], p["w_out"], p["g_ffn"], p["w_ff1"], p["w_ff2"], p["g_fin"])
    return out.reshape(bsz, seq, D_MODEL)


def _prepare_params(norm_mix, w_in, sgu_norm, w_spatial, b_spatial, q_norm, w_uq, kv_norm,
                    w_ukv, out_norm_a, out_norm_b, w_out, norm_ffn, w_ff1, w_ff2, norm_final):
    w_in_p = jnp.pad(w_in[0], ((0, 0), (0, IN_COLS_PAD - IN_COLS))).astype(_BF16)
    uq = w_uq[0].reshape(Q_RANK, B_HEADS, QK_DIM)
    uq = jnp.pad(uq, ((0, 0), (0, 0), (0, QK_PAD - QK_DIM)))
    row = lambda g: g.reshape(1, -1).astype(_F32)
    return dict(
        g_mix=row(norm_mix[0]), w_in=w_in_p, g_sgu=row(sgu_norm[0]),
        w_sp=w_spatial[0].astype(_BF16), b_sp=b_spatial[0].T.astype(_F32),
        g_q=row(q_norm[0]), w_uq=uq.reshape(Q_RANK, B_HEADS * QK_PAD).astype(_BF16),
        g_kv=row(kv_norm[0]), w_ukv=w_ukv[0].astype(_BF16), g_a=row(out_norm_a[0]),
        g_b=row(out_norm_b[0]), w_out=w_out[0].astype(_BF16), g_ffn=row(norm_ffn[0]),
        w_ff1=w_ff1[0].astype(_BF16), w_ff2=w_ff2[0].astype(_BF16), g_fin=row(norm_final))


def kernel(x_prompt, x_sample, norm_mix, w_in, sgu_norm, w_spatial, b_spatial, q_norm, w_uq,
           kv_norm, w_ukv, out_norm_a, out_norm_b, w_out, norm_ffn, w_ff1, w_ff2, norm_final):
    assert norm_mix.shape[0] == 1, "single-layer trunk"
    p = _prepare_params(norm_mix, w_in, sgu_norm, w_spatial, b_spatial, q_norm, w_uq, kv_norm,
                        w_ukv, out_norm_a, out_norm_b, w_out, norm_ffn, w_ff1, w_ff2,
                        norm_final)
    return (_trunk(x_prompt, p), _trunk(x_sample, p))
```

```python
import math

import jax
import jax.numpy as jnp
from jax.experimental import pallas as pl
from jax.experimental.pallas import tpu as pltpu

LANES = 128
D_MODEL = 1024
A_WIDTH = 512
A_HEADS = 4
A_HEAD_DIM = A_WIDTH // A_HEADS
CHUNK = 128
B_HEADS = 4
NOPE_DIM = 128
ROPE_DIM = 64
V_DIM = 128
QK_DIM = NOPE_DIM + ROPE_DIM
QK_PAD = 2 * LANES
VT_ROWS = V_DIM + 16
B_WIDTH = B_HEADS * V_DIM
Q_RANK = 384
KV_RANK = 256
IN_COLS = 2 * A_WIDTH + Q_RANK + KV_RANK + ROPE_DIM
IN_COLS_PAD = 2 * A_WIDTH + Q_RANK + KV_RANK + LANES
D_FF = 4 * D_MODEL
ROPE_BASE = 10000.0
EPS = 1e-6
ATTN_SCALE = QK_DIM ** -0.5
Q_SCALE = ATTN_SCALE * math.log2(math.e)
VMEM_LIMIT_BYTES = 56 * 1024 * 1024

MIX_ROWS = 1024
MIX_SUB = 256
FFN_TILE = 1024
FFN_SUB = 512
Q_TILE = 512
STEP_SCORES = 4 * 1024 * 1024
K_TILE = 512
K_SUB = 256
K_AHEAD = 2
HEAD_GROUP = 4
LAG_LIMIT = 64.0
assert MIX_ROWS % K_TILE == 0 and K_TILE % MIX_SUB == 0
FF_CHUNK = 1024

_BF16 = jnp.bfloat16
_F32 = jnp.float32


def _rms(x, gain):
    return x * jax.lax.rsqrt(jnp.mean(x * x, axis=-1, keepdims=True) + EPS) * gain


def _gelu_tanh(x):
    c = math.sqrt(2.0 / math.pi)
    return x * (0.5 + 0.5 * jnp.tanh(x * (c + (c * 0.044715) * (x * x))))


def _rope_padded(x, cos_t, sin_lo, sin_hi):
    half = ROPE_DIM // 2
    return (x * cos_t + pltpu.roll(x, LANES - half, 1) * sin_lo
            + pltpu.roll(x, half, 1) * sin_hi)


def _mix_in_kernel(x_ref, cos_ref, slo_ref, shi_ref, g_mix_ref, w_in_ref, g_sgu_ref,
                   w_sp_ref, b_sp_ref, g_q_ref, w_uq_ref, g_kv_ref, w_ukv_ref, g_a_ref,
                   ya_ref, q_ref, k_ref, vt_ref):
    starts = range(0, MIX_ROWS, MIX_SUB)
    zs = [jnp.dot(_rms(x_ref[r0:r0 + MIX_SUB, :], g_mix_ref[...]).astype(_BF16), w_in_ref[...],
                  preferred_element_type=_F32) for r0 in starts]
    for z, r0 in zip(zs, starts):
        rs = slice(r0, r0 + MIX_SUB)
        kt, off = divmod(r0, K_TILE)
        _mix_in_rows(z, cos_ref[rs, :], slo_ref[rs, :], shi_ref[rs, :], g_sgu_ref, w_sp_ref,
                     b_sp_ref, g_q_ref, w_uq_ref, g_kv_ref, w_ukv_ref, g_a_ref,
                     ya_ref.at[rs, :], q_ref.at[rs, :], k_ref.at[rs, :],
                     vt_ref.at[:, kt, :, off:off + MIX_SUB])


def _mix_in_rows(z, cos_t, sin_lo, sin_hi, g_sgu_ref, w_sp_ref, b_sp_ref, g_q_ref, w_uq_ref,
                 g_kv_ref, w_ukv_ref, g_a_ref, ya_ref, q_ref, k_ref, vt_ref):
    rows = z.shape[0]
    u = _gelu_tanh(z[:, :A_WIDTH])
    v = _gelu_tanh(z[:, A_WIDTH:2 * A_WIDTH])

    ya_cols = []
    for hd in range(A_HEADS):
        cols = slice(hd * A_HEAD_DIM, (hd + 1) * A_HEAD_DIM)
        vn = _rms(v[:, cols], g_sgu_ref[:, cols]).astype(_BF16)
        bias = jnp.broadcast_to(b_sp_ref[:, hd:hd + 1], (CHUNK, A_HEAD_DIM))
        parts = []
        for c in range(rows // CHUNK):
            s = jnp.dot(w_sp_ref[hd], vn[c * CHUNK:(c + 1) * CHUNK],
                        preferred_element_type=_F32) + bias
            parts.append(s)
        ya_cols.append(u[:, cols] * jnp.concatenate(parts, axis=0))
    ya = jnp.concatenate(ya_cols, axis=1)
    ya_ref[...] = _rms(ya, g_a_ref[...]).astype(_BF16)

    o = 2 * A_WIDTH
    cq = _rms(z[:, o:o + Q_RANK], g_q_ref[...]).astype(_BF16)
    q = jnp.dot(cq, w_uq_ref[...], preferred_element_type=_F32)
    for hd in range(B_HEADS):
        base = hd * QK_PAD
        q_ref[:, base:base + LANES] = (q[:, base:base + LANES] * Q_SCALE).astype(_BF16)
        pe = _rope_padded(q[:, base + LANES:base + QK_PAD], cos_t, sin_lo, sin_hi)
        q_ref[:, base + LANES:base + QK_PAD] = (pe * Q_SCALE).astype(_BF16)

    o += Q_RANK
    ckv = _rms(z[:, o:o + KV_RANK], g_kv_ref[...]).astype(_BF16)
    kv = jnp.dot(ckv, w_ukv_ref[...], preferred_element_type=_F32)
    o += KV_RANK
    k_pe = _rope_padded(z[:, o:o + LANES], cos_t, sin_lo, sin_hi).astype(_BF16)
    for hd in range(B_HEADS):
        src = hd * (NOPE_DIM + V_DIM)
        k_ref[:, hd * QK_PAD:hd * QK_PAD + LANES] = kv[:, src:src + NOPE_DIM].astype(_BF16)
        k_ref[:, hd * QK_PAD + LANES:(hd + 1) * QK_PAD] = k_pe
        vt_ref[hd, :V_DIM, :] = kv[:, src + NOPE_DIM:src + NOPE_DIM + V_DIM].T.astype(_BF16)
        vt_ref[hd, V_DIM:, :] = jnp.ones((VT_ROWS - V_DIM, rows), _BF16)


def _scores_t(qt, k_ref, hd, row0, n_rows):
    if not isinstance(row0, int):
        row0 = pl.multiple_of(row0, n_rows)
    return jnp.dot(k_ref[pl.ds(row0, n_rows), hd * QK_PAD:(hd + 1) * QK_PAD], qt,
                   preferred_element_type=_F32)


def _store_head(o_ref, s, hd, acc):
    o_ref[s * Q_TILE:(s + 1) * Q_TILE, hd * V_DIM:(hd + 1) * V_DIM] = (
        acc[:V_DIM] / acc[V_DIM:V_DIM + 1]).T.astype(o_ref.dtype)


def _attend_kernel(q_ref, k_ref, vt_ref, o_ref, qt_ref):
    n_tiles = k_ref.shape[0] // K_TILE
    n_sub = K_TILE // K_SUB
    n_streams = q_ref.shape[0] // Q_TILE
    units = [(s, hd, t, u) for s in range(n_streams) for g in range(0, B_HEADS, HEAD_GROUP)
             for t in range(n_tiles) for u in range(n_sub) for hd in range(g, g + HEAD_GROUP)]

    for s in range(n_streams):
        for hd in range(B_HEADS):
            qt_ref[s, hd] = q_ref[s * Q_TILE:(s + 1) * Q_TILE, hd * QK_PAD:(hd + 1) * QK_PAD].T

    def scores_u(s, hd, t, u):
        return _scores_t(qt_ref[s, hd], k_ref, hd, t * K_TILE + u * K_SUB, K_SUB)

    peak = jnp.zeros((1, Q_TILE), _F32)
    shift, acc = {}, {}
    ahead = [scores_u(*un) for un in units[:K_AHEAD]]
    for i, (s, hd, t, u) in enumerate(units):
        st = ahead.pop(0)
        if i + K_AHEAD < len(units):
            ahead.append(scores_u(*units[i + K_AHEAD]))
        first = t == 0 and u == 0
        if first:
            shift[s, hd] = jnp.max(st, axis=0, keepdims=True)
        p = jnp.exp2(st - shift[s, hd]).astype(_BF16)
        if not first:
            peak = jnp.maximum(peak, jnp.max(p, axis=0, keepdims=True).astype(_F32))
        pv = jnp.dot(vt_ref[hd, t, :, u * K_SUB:(u + 1) * K_SUB], p,
                     preferred_element_type=_F32)
        acc[s, hd] = pv if first else acc[s, hd] + pv
        if t == n_tiles - 1 and u == n_sub - 1:
            _store_head(o_ref, s, hd, acc.pop((s, hd)))

    @pl.when(jnp.max(peak) > 2.0 ** LAG_LIMIT)
    def _():
        for s in range(n_streams):
            for hd in range(B_HEADS):
                def tile_step(t, carry, s=s, hd=hd):
                    m, acc = carry
                    st = _scores_t(qt_ref[s, hd], k_ref, hd, t * K_TILE, K_TILE)
                    m_new = jnp.maximum(m, jnp.max(st, axis=0, keepdims=True))
                    acc = jnp.exp2(m - m_new) * acc + jnp.dot(
                        vt_ref[hd, t], jnp.exp2(st - m_new).astype(_BF16),
                        preferred_element_type=_F32)
                    return m_new, acc

                init = (jnp.full((1, Q_TILE), -jnp.inf, _F32),
                        jnp.zeros((VT_ROWS, Q_TILE), _F32))
                _store_head(o_ref, s, hd, jax.lax.fori_loop(0, n_tiles, tile_step, init)[1])


def _ffn_out_kernel(x_ref, ya_ref, yb_ref, g_b_ref, w_out_ref, g_ffn_ref, w1_ref, w2_ref,
                    g_fin_ref, o_ref, h_ref):
    subs = [slice(r0, r0 + FFN_SUB) for r0 in range(0, FFN_TILE, FFN_SUB)]
    for rs in subs:
        yb = _rms(yb_ref[rs, :], g_b_ref[...]).astype(_BF16)
        x1 = (x_ref[rs, :]
              + jnp.dot(ya_ref[rs, :], w_out_ref[:A_WIDTH, :], preferred_element_type=_F32)
              + jnp.dot(yb, w_out_ref[A_WIDTH:, :], preferred_element_type=_F32))
        o_ref[rs, :] = x1
        h_ref[rs, :] = _rms(x1, g_ffn_ref[...]).astype(_BF16)
    for rs in subs:
        for c in range(0, D_FF, FF_CHUNK):
            f = jnp.maximum(jnp.dot(h_ref[rs, :], w1_ref[:, c:c + FF_CHUNK],
                                    preferred_element_type=_F32), 0.0)
            o_ref[rs, :] += jnp.dot((f * f).astype(_BF16), w2_ref[c:c + FF_CHUNK, :],
                                    preferred_element_type=_F32)
        o_ref[rs, :] = _rms(o_ref[rs, :], g_fin_ref[...])


def _const_spec(shape):
    nd = len(shape)
    return pl.BlockSpec(shape, lambda *_: (0,) * nd, pipeline_mode=pl.Buffered(1))


def _rope_tables(seq):
    inv = 1.0 / (ROPE_BASE ** (jnp.arange(0, ROPE_DIM, 2, dtype=_F32) / ROPE_DIM))
    ang = jnp.arange(seq, dtype=_F32)[:, None] * inv[None, :]
    cos, sin = jnp.cos(ang), jnp.sin(ang)
    zero = jnp.zeros_like(cos)
    pad = jnp.zeros((seq, LANES - ROPE_DIM), _F32)
    cos_t = jnp.concatenate([cos, cos, pad], axis=1)
    sin_lo = jnp.concatenate([-sin, zero, pad], axis=1)
    sin_hi = jnp.concatenate([zero, sin, pad], axis=1)
    return cos_t, sin_lo, sin_hi


def _attend(q, k, vt):
    bsz, _, k_tiles, _, _ = vt.shape
    seq = k_tiles * K_TILE
    q_rows = min(seq, STEP_SCORES // seq)
    assert q_rows % Q_TILE == 0 and seq % q_rows == 0
    q_tiles = seq // q_rows
    return pl.pallas_call(
        _attend_kernel,
        grid=(bsz, q_tiles),
        in_specs=[pl.BlockSpec((q_rows, B_HEADS * QK_PAD), lambda b, i: (b * q_tiles + i, 0)),
                  pl.BlockSpec((seq, B_HEADS * QK_PAD), lambda b, i: (b, 0)),
                  pl.BlockSpec((None, B_HEADS, k_tiles, VT_ROWS, K_TILE),
                               lambda b, i: (b, 0, 0, 0, 0))],
        out_specs=pl.BlockSpec((q_rows, B_WIDTH), lambda b, i: (b * q_tiles + i, 0)),
        out_shape=jax.ShapeDtypeStruct((bsz * seq, B_WIDTH), _F32),
        scratch_shapes=[pltpu.VMEM((q_rows // Q_TILE, B_HEADS, QK_PAD, Q_TILE), _BF16)],
        compiler_params=pltpu.CompilerParams(
            dimension_semantics=("arbitrary", "arbitrary"),
            vmem_limit_bytes=VMEM_LIMIT_BYTES),
        name="attend",
    )(q, k, vt)


def _trunk(x, p):
    bsz, seq, _ = x.shape
    n_tok = bsz * seq
    x2 = x.reshape(n_tok, D_MODEL)
    steps_per_seq = seq // MIX_ROWS
    cos_t, sin_lo, sin_hi = _rope_tables(seq)

    tok = lambda width: pl.BlockSpec((MIX_ROWS, width), lambda i: (i, 0))
    rope = pl.BlockSpec((MIX_ROWS, LANES), lambda i: (i % steps_per_seq, 0))
    params = pltpu.CompilerParams(dimension_semantics=("arbitrary",),
                                  vmem_limit_bytes=VMEM_LIMIT_BYTES)

    ya, q, k, vt = pl.pallas_call(
        _mix_in_kernel,
        grid=(n_tok // MIX_ROWS,),
        in_specs=[tok(D_MODEL), rope, rope, rope,
                  _const_spec((1, D_MODEL)), _const_spec((D_MODEL, IN_COLS_PAD)),
                  _const_spec((1, A_WIDTH)), _const_spec((A_HEADS, CHUNK, CHUNK)),
                  _const_spec((CHUNK, A_HEADS)), _const_spec((1, Q_RANK)),
                  _const_spec((Q_RANK, B_HEADS * QK_PAD)), _const_spec((1, KV_RANK)),
                  _const_spec((KV_RANK, B_HEADS * (NOPE_DIM + V_DIM))),
                  _const_spec((1, A_WIDTH))],
        out_specs=[tok(A_WIDTH), tok(B_HEADS * QK_PAD), tok(B_HEADS * QK_PAD),
                   pl.BlockSpec((None, B_HEADS, MIX_ROWS // K_TILE, VT_ROWS, K_TILE),
                                lambda i: (i // steps_per_seq, 0, i % steps_per_seq, 0, 0))],
        out_shape=[jax.ShapeDtypeStruct((n_tok, A_WIDTH), _BF16),
                   jax.ShapeDtypeStruct((n_tok, B_HEADS * QK_PAD), _BF16),
                   jax.ShapeDtypeStruct((n_tok, B_HEADS * QK_PAD), _BF16),
                   jax.ShapeDtypeStruct((bsz, B_HEADS, seq // K_TILE, VT_ROWS, K_TILE),
                                        _BF16)],
        compiler_params=params,
        name="mix_in",
    )(x2, cos_t, sin_lo, sin_hi, p["g_mix"], p["w_in"], p["g_sgu"], p["w_sp"], p["b_sp"],
      p["g_q"], p["w_uq"], p["g_kv"], p["w_ukv"], p["g_a"])

    yb = _attend(q, k, vt)

    ftok = lambda width: pl.BlockSpec((FFN_TILE, width), lambda i: (i, 0))
    out = pl.pallas_call(
        _ffn_out_kernel,
        grid=(n_tok // FFN_TILE,),
        in_specs=[ftok(D_MODEL), ftok(A_WIDTH), ftok(B_WIDTH), _const_spec((1, B_WIDTH)),
                  _const_spec((D_MODEL, D_MODEL)), _const_spec((1, D_MODEL)),
                  _const_spec((D_MODEL, D_FF)), _const_spec((D_FF, D_MODEL)),
                  _const_spec((1, D_MODEL))],
        out_specs=ftok(D_MODEL),
        out_shape=jax.ShapeDtypeStruct((n_tok, D_MODEL), _F32),
        scratch_shapes=[pltpu.VMEM((FFN_TILE, D_MODEL), _BF16)],
        compiler_params=params,
        name="ffn_out",
    )(x2, ya, yb, p["g_b"], p["w_out"], p["g_ffn"], p["w_ff1"], p["w_ff2"], p["g_fin"])
    return out.reshape(bsz, seq, D_MODEL)


def _prepare_params(norm_mix, w_in, sgu_norm, w_spatial, b_spatial, q_norm, w_uq, kv_norm,
                    w_ukv, out_norm_a, out_norm_b, w_out, norm_ffn, w_ff1, w_ff2, norm_final):
    w_in_p = jnp.pad(w_in[0], ((0, 0), (0, IN_COLS_PAD - IN_COLS))).astype(_BF16)
    uq = w_uq[0].reshape(Q_RANK, B_HEADS, QK_DIM)
    uq = jnp.pad(uq, ((0, 0), (0, 0), (0, QK_PAD - QK_DIM)))
    row = lambda g: g.reshape(1, -1).astype(_F32)
    return dict(
        g_mix=row(norm_mix[0]), w_in=w_in_p, g_sgu=row(sgu_norm[0]),
        w_sp=w_spatial[0].astype(_BF16), b_sp=b_spatial[0].T.astype(_F32),
        g_q=row(q_norm[0]), w_uq=uq.reshape(Q_RANK, B_HEADS * QK_PAD).astype(_BF16),
        g_kv=row(kv_norm[0]), w_ukv=w_ukv[0].astype(_BF16), g_a=row(out_norm_a[0]),
        g_b=row(out_norm_b[0]), w_out=w_out[0].astype(_BF16), g_ffn=row(norm_ffn[0]),
        w_ff1=w_ff1[0].astype(_BF16), w_ff2=w_ff2[0].astype(_BF16), g_fin=row(norm_final))


def kernel(x_prompt, x_sample, norm_mix, w_in, sgu_norm, w_spatial, b_spatial, q_norm, w_uq,
           kv_norm, w_ukv, out_norm_a, out_norm_b, w_out, norm_ffn, w_ff1, w_ff2, norm_final):
    assert norm_mix.shape[0] == 1, "single-layer trunk"
    p = _prepare_params(norm_mix, w_in, sgu_norm, w_spatial, b_spatial, q_norm, w_uq, kv_norm,
                        w_ukv, out_norm_a, out_norm_b, w_out, norm_ffn, w_ff1, w_ff2,
                        norm_final)
    return (_trunk(x_prompt, p), _trunk(x_sample, p))
```
